```python
import jax, jax.numpy as jnp
from jax import lax
import numpy as np

D_MODEL = 1024
BATCH = 4
SEQ = 8192
DEPTH = 2

N_META = 16
N_HEADS = 16
HEAD_DIM = 64
N_KV_HEADS = 4
ATTN_WIDTH = N_HEADS * HEAD_DIM
KV_WIDTH = N_KV_HEADS * HEAD_DIM
IDX_HEADS = 8
IDX_DIM = 64
TOPK_MAX = 256
Q_BLOCK = 128
ROPE_THETA = 500000.0
ROT_DIM = HEAD_DIM // 4
CONV_CH = D_MODEL
CONV_WIDTH = 31
FFN_HIDDEN = 2816
FFN_CONV_WIDTH = 3
RMS_EPS = 1e-6
LN_EPS = 1e-5
SPLIT_SIZES = (ATTN_WIDTH, KV_WIDTH, KV_WIDTH, IDX_HEADS * IDX_DIM, IDX_DIM, IDX_HEADS,
               2 * CONV_CH, D_MODEL, D_MODEL)
IN_COLS = int(sum(SPLIT_SIZES))
SPLIT_POINTS = tuple(int(v) for v in np.cumsum(SPLIT_SIZES)[:-1])

kernel_name = "hybrid_dsa_conformer_gated_trunk"


def rmsnorm(x, g):
    x32 = x.astype(jnp.float32)
    y = x32 * lax.rsqrt(jnp.mean(x32 * x32, axis=-1, keepdims=True) + RMS_EPS)
    return (y * g.astype(jnp.float32)).astype(x.dtype)


def layernorm(x, g, b):
    x32 = x.astype(jnp.float32)
    mu = jnp.mean(x32, axis=-1, keepdims=True)
    var = jnp.mean(jnp.square(x32 - mu), axis=-1, keepdims=True)
    y = (x32 - mu) * lax.rsqrt(var + LN_EPS)
    return (y * g.astype(jnp.float32) + b.astype(jnp.float32)).astype(x.dtype)


def causal_dwconv(x, w, b):
    K, C = w.shape
    y = lax.conv_general_dilated(x, w[:, None, :].astype(x.dtype), window_strides=(1,),
                                 padding=[(K - 1, 0)],
                                 dimension_numbers=('NWC', 'WIO', 'NWC'),
                                 feature_group_count=C)
    return y + b.astype(x.dtype)


def rope_tables(T):
    pos = jnp.arange(T, dtype=jnp.float32)
    inv_freq = jnp.power(jnp.float32(ROPE_THETA),
                         -jnp.arange(0, ROT_DIM, 2, dtype=jnp.float32) / ROT_DIM)
    ang = pos[:, None] * inv_freq[None, :]
    return jnp.cos(ang), jnp.sin(ang)


def apply_partial_rope(x, cos, sin):
    half = ROT_DIM // 2
    x1 = x[..., :half].astype(jnp.float32)
    x2 = x[..., half:ROT_DIM].astype(jnp.float32)
    c = cos[:, None, :]
    s = sin[:, None, :]
    rot = jnp.concatenate([x1 * c - x2 * s, x2 * c + x1 * s], axis=-1).astype(x.dtype)
    return jnp.concatenate([rot, x[..., ROT_DIM:]], axis=-1)


def dsa_attention(q, k, v, qi, ki, wi):
    B, T = q.shape[0], q.shape[1]
    k_sel = min(TOPK_MAX, T // 4)
    n_blocks = -(-T // Q_BLOCK)
    pad = n_blocks * Q_BLOCK - T
    q = jnp.pad(q, ((0, 0), (0, pad), (0, 0), (0, 0)))
    qi = jnp.pad(qi, ((0, 0), (0, pad), (0, 0), (0, 0)))
    wi = jnp.pad(wi, ((0, 0), (0, pad), (0, 0)))
    key_pos = jnp.arange(T)
    rep = N_HEADS // N_KV_HEADS
    attn_scale = HEAD_DIM ** -0.5
    idx_scale = IDX_DIM ** -0.5
    gather = jax.vmap(lambda table, idx: table[idx])

    def block(i):
        start = i * Q_BLOCK
        qb = lax.dynamic_slice_in_dim(q, start, Q_BLOCK, axis=1)
        qib = lax.dynamic_slice_in_dim(qi, start, Q_BLOCK, axis=1)
        wib = lax.dynamic_slice_in_dim(wi, start, Q_BLOCK, axis=1)
        q_pos = start + jnp.arange(Q_BLOCK)
        causal = key_pos[None, :] <= q_pos[:, None]
        logits = jnp.einsum('bqhd,bkd->bhqk', qib, ki).astype(jnp.float32) * idx_scale
        score = jnp.einsum('bhqk,bqh->bqk', jax.nn.relu(logits), wib.astype(jnp.float32))
        score = jnp.where(causal[None], score, -jnp.inf)
        _, sel = lax.top_k(score, k_sel)
        valid = sel <= q_pos[None, :, None]
        kg = gather(k, sel)
        vg = gather(v, sel)
        qg = qb.reshape(B, Q_BLOCK, N_KV_HEADS, rep, HEAD_DIM)
        s = jnp.einsum('bqgrd,bqsgd->bqgrs', qg, kg).astype(jnp.float32) * attn_scale
        s = jnp.where(valid[:, :, None, None, :], s, -jnp.inf)
        p = jax.nn.softmax(s, axis=-1).astype(vg.dtype)
        o = jnp.einsum('bqgrs,bqsgd->bqgrd', p, vg)
        return o.reshape(B, Q_BLOCK, ATTN_WIDTH)

    out = lax.map(block, jnp.arange(n_blocks))
    out = jnp.transpose(out, (1, 0, 2, 3)).reshape(B, n_blocks * Q_BLOCK, ATTN_WIDTH)
    return out[:, :T]


def hybrid_mixer(xn, cos, sin, w_in, w_attn_out, conv_dw_w, conv_dw_b, conv_ln_g, conv_ln_b,
                 w_conv_out, w_o):
    B, T, _ = xn.shape
    proj = xn @ w_in
    q, k, v, qi, ki, wi, glu_in, gate_a, gate_c = jnp.split(proj, SPLIT_POINTS, axis=-1)
    q = apply_partial_rope(q.reshape(B, T, N_HEADS, HEAD_DIM), cos, sin)
    k = apply_partial_rope(k.reshape(B, T, N_KV_HEADS, HEAD_DIM), cos, sin)
    v = v.reshape(B, T, N_KV_HEADS, HEAD_DIM)
    qi = apply_partial_rope(qi.reshape(B, T, IDX_HEADS, IDX_DIM), cos, sin)
    ki = apply_partial_rope(ki.reshape(B, T, 1, IDX_DIM), cos, sin)[:, :, 0]
    wi = wi * (IDX_HEADS ** -0.5)
    y_attn = dsa_attention(q, k, v, qi, ki, wi) @ w_attn_out
    a, b = jnp.split(glu_in, 2, axis=-1)
    u = a * jax.nn.sigmoid(b)
    u = causal_dwconv(u, conv_dw_w, conv_dw_b)
    u = jax.nn.silu(layernorm(u, conv_ln_g, conv_ln_b))
    y_conv = u @ w_conv_out
    merged = jax.nn.sigmoid(gate_a) * y_attn + jax.nn.sigmoid(gate_c) * y_conv
    return merged @ w_o


def conv_ffn(xn, w_up, ffn_dw_w, ffn_dw_b, w_down):
    h = causal_dwconv(xn @ w_up, ffn_dw_w, ffn_dw_b)
    g, u = jnp.split(h, 2, axis=-1)
    return (jax.nn.silu(g) * u) @ w_down


def setup_inputs(seed: int = 0) -> dict:
    key = jax.random.key(seed)
    ks = jax.random.split(key, 20)
    f32 = jnp.float32
    n = lambda k, shape, s: jax.random.normal(k, shape, f32) * s
    return {
        "x": n(ks[0], (BATCH, SEQ, D_MODEL), 1.0),
        "meta_tokens": n(ks[1], (N_META, D_MODEL), 1.0),
        "attn_norm_g": 1.0 + n(ks[2], (DEPTH, D_MODEL), 0.01),
        "w_in": n(ks[3], (DEPTH, D_MODEL, IN_COLS), D_MODEL ** -0.5),
        "w_attn_out": n(ks[4], (DEPTH, ATTN_WIDTH, D_MODEL), ATTN_WIDTH ** -0.5),
        "conv_dw_w": n(ks[5], (DEPTH, CONV_WIDTH, CONV_CH), CONV_WIDTH ** -0.5),
        "conv_dw_b": n(ks[6], (DEPTH, CONV_CH), 0.01),
        "conv_ln_g": 1.0 + n(ks[7], (DEPTH, CONV_CH), 0.01),
        "conv_ln_b": n(ks[8], (DEPTH, CONV_CH), 0.01),
        "w_conv_out": n(ks[9], (DEPTH, CONV_CH, D_MODEL), CONV_CH ** -0.5),
        "w_o": n(ks[10], (DEPTH, D_MODEL, D_MODEL), D_MODEL ** -0.5),
        "ffn_norm_g": 1.0 + n(ks[11], (DEPTH, D_MODEL), 0.01),
        "w_up": n(ks[12], (DEPTH, D_MODEL, 2 * FFN_HIDDEN), D_MODEL ** -0.5),
        "ffn_dw_w": n(ks[13], (DEPTH, FFN_CONV_WIDTH, 2 * FFN_HIDDEN), FFN_CONV_WIDTH ** -0.5),
        "ffn_dw_b": n(ks[14], (DEPTH, 2 * FFN_HIDDEN), 0.01),
        "w_down": n(ks[15], (DEPTH, FFN_HIDDEN, D_MODEL), FFN_HIDDEN ** -0.5),
        "final_norm_g": 1.0 + n(ks[16], (D_MODEL,), 0.01),
    }


def reference(x, meta_tokens, attn_norm_g, w_in, w_attn_out, conv_dw_w, conv_dw_b, conv_ln_g,
              conv_ln_b, w_conv_out, w_o, ffn_norm_g, w_up, ffn_dw_w, ffn_dw_b, w_down,
              final_norm_g):
    B = x.shape[0]
    T = N_META + x.shape[1]
    meta = jnp.broadcast_to(meta_tokens[None].astype(x.dtype), (B, N_META, D_MODEL))
    h = jnp.concatenate([meta, x], axis=1)
    cos, sin = rope_tables(T)
    for l in range(DEPTH):
        h = h + hybrid_mixer(rmsnorm(h, attn_norm_g[l]), cos, sin, w_in[l], w_attn_out[l],
                             conv_dw_w[l], conv_dw_b[l], conv_ln_g[l], conv_ln_b[l],
                             w_conv_out[l], w_o[l])
        h = h + conv_ffn(rmsnorm(h, ffn_norm_g[l]), w_up[l], ffn_dw_w[l], ffn_dw_b[l], w_down[l])
    h = rmsnorm(h, final_norm_g)
    return h[:, N_META:]
```

```python
import functools

import jax
import jax.numpy as jnp
from jax import lax
from jax.experimental import pallas as pl
from jax.experimental.pallas import tpu as pltpu

D_MODEL = 1024
N_META = 16
N_HEADS = 16
HEAD_DIM = 64
N_KV_HEADS = 4
HEADS_PER_KV = N_HEADS // N_KV_HEADS
IDX_HEADS = 8
IDX_DIM = 64
TOPK_MAX = 256
ROPE_THETA = 500000.0
ROT_DIM = HEAD_DIM // 4
CONV_WIDTH = 31
FFN_HIDDEN = 2816
FFN_CONV_WIDTH = 3
RMS_EPS = 1e-6
LN_EPS = 1e-5

LANES = 128
Q_TILE = 128
K_TILE = 640
ROW_TILE = 640
MIX_TILE = 320
FFN_TILE = 320
CONV_HALO = 32
FFN_HALO = 16
FFN_CHUNK = 256
N_BISECT = 14
VMEM_LIMIT = 56 * 1024 * 1024
MXU_DTYPE = jnp.bfloat16

_C_Q = 0
_C_K = _C_Q + N_HEADS * HEAD_DIM
_C_QI = _C_K + N_KV_HEADS * LANES
_C_KI = _C_QI + IDX_HEADS * IDX_DIM
_C_V = _C_KI + LANES
_C_WI = _C_V + N_KV_HEADS * LANES
_C_GLU = _C_WI + LANES
_C_GATE = _C_GLU + 2 * D_MODEL
_C_END = _C_GATE + 2 * D_MODEL
_N_ROPE_COLS = _C_V


def _sigmoid(x):
    return 1.0 / (1.0 + jnp.exp(-x))


def _dot(a, b):
    return jnp.dot(a, b, preferred_element_type=jnp.float32)


def _proj_kernel(h_ref, g_ref, w_ref, cos_ref, sina_ref, sinb_ref,
                 q_ref, kt_ref, qi_ref, kit_ref, v_ref, wi_ref, u_ref, ga_ref, gc_ref):
    x = h_ref[0]
    ms = jnp.mean(x * x, axis=-1, keepdims=True)
    xn = (x * lax.rsqrt(ms + RMS_EPS) * g_ref[...]).astype(MXU_DTYPE)
    cos = cos_ref[...]
    sina = sina_ref[...]
    sinb = sinb_ref[...]

    def rope(r):
        return (r * cos + pltpu.roll(r, LANES - ROT_DIM // 2, 1) * sina
                + pltpu.roll(r, ROT_DIM // 2, 1) * sinb)

    def proj(c0, width):
        return _dot(xn, w_ref[:, c0:c0 + width])

    for c0 in range(_C_Q, _C_K, 2 * LANES):
        r = proj(c0, 2 * LANES)
        for s in range(2):
            q_ref[0, :, c0 + s * LANES:c0 + (s + 1) * LANES] = rope(
                r[:, s * LANES:(s + 1) * LANES]).astype(q_ref.dtype)
    for c0 in range(_C_K, _C_QI, 2 * LANES):
        r = proj(c0, 2 * LANES)
        for s in range(2):
            row = c0 - _C_K + s * LANES
            kt_ref[0, 0, row:row + LANES, :] = rope(
                r[:, s * LANES:(s + 1) * LANES]).T.astype(kt_ref.dtype)
    for c0 in range(_C_QI, _C_KI, 2 * LANES):
        r = proj(c0, 2 * LANES)
        for s in range(2):
            col = c0 - _C_QI + s * LANES
            qi_ref[0, :, col:col + LANES] = rope(
                r[:, s * LANES:(s + 1) * LANES]).astype(qi_ref.dtype)
    kit_ref[0, 0] = rope(proj(_C_KI, LANES)).T.astype(kit_ref.dtype)
    for c0 in range(_C_V, _C_WI, 2 * LANES):
        v_ref[0, :, c0 - _C_V:c0 - _C_V + 2 * LANES] = proj(c0, 2 * LANES).astype(v_ref.dtype)
    wi_ref[0] = proj(_C_WI, LANES) * (IDX_HEADS ** -0.5)
    for c0 in range(0, D_MODEL, 2 * LANES):
        a = proj(_C_GLU + c0, 2 * LANES)
        b = proj(_C_GLU + D_MODEL + c0, 2 * LANES)
        u_ref[0, :, c0:c0 + 2 * LANES] = (a * _sigmoid(b)).astype(u_ref.dtype)
    for c0 in range(0, D_MODEL, 2 * LANES):
        ga_ref[0, :, c0:c0 + 2 * LANES] = _sigmoid(proj(_C_GATE + c0, 2 * LANES)).astype(ga_ref.dtype)
        gc_ref[0, :, c0:c0 + 2 * LANES] = _sigmoid(
            proj(_C_GATE + D_MODEL + c0, 2 * LANES)).astype(gc_ref.dtype)


def _proj(h, g, w, cos, sina, sinb):
    B, Tp, D = h.shape
    nt = Tp // ROW_TILE
    bf = MXU_DTYPE
    row = lambda width: pl.BlockSpec((1, ROW_TILE, width), lambda b, i: (b, i, 0))
    tab = pl.BlockSpec((ROW_TILE, LANES), lambda b, i: (i, 0))
    out_shape = (
        jax.ShapeDtypeStruct((B, Tp, N_HEADS * HEAD_DIM), bf),
        jax.ShapeDtypeStruct((B, nt, N_KV_HEADS * LANES, ROW_TILE), bf),
        jax.ShapeDtypeStruct((B, Tp, IDX_HEADS * IDX_DIM), bf),
        jax.ShapeDtypeStruct((B, nt, LANES, ROW_TILE), bf),
        jax.ShapeDtypeStruct((B, Tp, N_KV_HEADS * LANES), bf),
        jax.ShapeDtypeStruct((B, Tp, LANES), jnp.float32),
        jax.ShapeDtypeStruct((B, Tp, D), bf),
        jax.ShapeDtypeStruct((B, Tp, D), bf),
        jax.ShapeDtypeStruct((B, Tp, D), bf),
    )
    out_specs = (
        row(N_HEADS * HEAD_DIM),
        pl.BlockSpec((1, 1, N_KV_HEADS * LANES, ROW_TILE), lambda b, i: (b, i, 0, 0)),
        row(IDX_HEADS * IDX_DIM),
        pl.BlockSpec((1, 1, LANES, ROW_TILE), lambda b, i: (b, i, 0, 0)),
        row(N_KV_HEADS * LANES),
        row(LANES),
        row(D), row(D), row(D),
    )
    return pl.pallas_call(
        _proj_kernel,
        grid=(B, nt),
        in_specs=[
            row(D),
            pl.BlockSpec((1, D), lambda b, i: (0, 0)),
            pl.BlockSpec((D, _C_END), lambda b, i: (0, 0), pipeline_mode=pl.Buffered(1)),
            tab, tab, tab,
        ],
        out_specs=out_specs,
        out_shape=out_shape,
        compiler_params=pltpu.CompilerParams(
            dimension_semantics=("arbitrary", "arbitrary"), vmem_limit_bytes=VMEM_LIMIT),
        name="proj",
    )(h, g, w, cos, sina, sinb)


def _attn_kernel(q_ref, qi_ref, wi_ref, kt_ref, kit_ref, v_ref, tri_ref, o_ref,
                 sc_ref, qis_ref, wst_ref, qst_ref, m_ref, l_ref, acc_ref, bias_ref, tie_ref):
    f32 = jnp.float32
    i = pl.program_id(1)
    n_kt = (i * Q_TILE + Q_TILE + K_TILE - 1) // K_TILE
    qpos = i * Q_TILE + lax.broadcasted_iota(jnp.int32, (Q_TILE, 1), 0)
    lane = lax.broadcasted_iota(jnp.int32, (Q_TILE, LANES), 1)
    low_half = lane < HEAD_DIM
    neg_inf = f32(-jnp.inf)
    pos_inf = f32(jnp.inf)
    n_chunks = K_TILE // LANES

    for hd in range(IDX_HEADS):
        blk = qi_ref[0, :, (hd // 2) * LANES:(hd // 2 + 1) * LANES]
        keep = low_half if hd % 2 == 0 else jnp.logical_not(low_half)
        qis_ref[hd * Q_TILE:(hd + 1) * Q_TILE, :] = jnp.where(keep, blk, jnp.zeros_like(blk))
        wst_ref[hd * Q_TILE:(hd + 1) * Q_TILE, :] = wi_ref[0, :, hd:hd + 1]

    def score_tile(j, carry):
        rmax, rmin = carry
        lg = _dot(qis_ref[...], kit_ref[0, j])
        r = jnp.maximum(lg, 0.0) * wst_ref[...]
        s = jnp.sum(r.reshape(IDX_HEADS, Q_TILE, K_TILE), axis=0)
        kpos = j * K_TILE + lax.broadcasted_iota(jnp.int32, (1, K_TILE), 1)
        causal = kpos <= qpos
        sc_ref[j] = jnp.where(causal, s, neg_inf)
        rmax = jnp.maximum(rmax, jnp.max(jnp.where(causal, s, neg_inf), axis=1, keepdims=True))
        rmin = jnp.minimum(rmin, jnp.min(jnp.where(causal, s, pos_inf), axis=1, keepdims=True))
        return rmax, rmin

    rmax, rmin = lax.fori_loop(
        0, n_kt, score_tile,
        (jnp.full((Q_TILE, 1), neg_inf, f32), jnp.full((Q_TILE, 1), pos_inf, f32)))

    def lane_fold(t):
        acc = t[:, 0:LANES]
        for c in range(1, n_chunks):
            acc = acc + t[:, c * LANES:(c + 1) * LANES]
        return acc

    def count_where(pred_fn):
        def body(j, acc):
            return acc + lane_fold(jnp.where(pred_fn(sc_ref[j]), 1.0, 0.0))
        acc = lax.fori_loop(0, n_kt, body, jnp.zeros((Q_TILE, LANES), f32))
        return jnp.sum(acc, axis=1, keepdims=True)

    def max_below(bound):
        def body(j, acc):
            t = sc_ref[j]
            t = jnp.where(t < bound, t, neg_inf)
            m = t[:, 0:LANES]
            for c in range(1, n_chunks):
                m = jnp.maximum(m, t[:, c * LANES:(c + 1) * LANES])
            return jnp.maximum(acc, m)
        acc = lax.fori_loop(0, n_kt, body, jnp.full((Q_TILE, LANES), neg_inf, f32))
        return jnp.max(acc, axis=1, keepdims=True)

    k_sel = f32(TOPK_MAX)
    active = qpos >= TOPK_MAX

    def bisect(_, carry):
        lo, hi, c_hi, strict = carry
        mid = lo + 0.5 * (hi - lo)
        c = count_where(lambda t: t >= mid)
        ge = c >= k_sel
        return (jnp.where(ge, mid, lo), jnp.where(ge, hi, mid),
                jnp.where(ge, c_hi, c), jnp.where(ge, strict, 1.0))

    lo, hi, c_hi, strict = lax.fori_loop(
        0, N_BISECT, bisect,
        (rmin, rmax, jnp.zeros((Q_TILE, 1), f32), jnp.zeros((Q_TILE, 1), f32)))

    def walk_cond(carry):
        done = carry[0]
        return jnp.min(done) < 0.5

    def walk_body(carry):
        done, bound, cnt, thr, need, n_eq = carry
        x = max_below(bound)
        e = count_where(lambda t: t == x)
        finished = jnp.logical_or(cnt + e >= k_sel, x == neg_inf)
        newly = jnp.logical_and(done < 0.5, finished)
        thr = jnp.where(newly, x, thr)
        need = jnp.where(newly, k_sel - cnt, need)
        n_eq = jnp.where(newly, e, n_eq)
        done = jnp.where(newly, 1.0, done)
        still = done < 0.5
        return (done, jnp.where(still, x, bound), jnp.where(still, cnt + e, cnt), thr, need, n_eq)

    is_strict = strict > 0.5
    done0 = jnp.where(active, 0.0, 1.0)
    _, _, _, thr, need, n_eq = lax.while_loop(
        walk_cond, walk_body,
        (done0, jnp.where(is_strict, hi, pos_inf), jnp.where(is_strict, c_hi, 0.0),
         jnp.zeros((Q_TILE, 1), f32), jnp.zeros((Q_TILE, 1), f32), jnp.zeros((Q_TILE, 1), f32)))
    thr = jnp.where(active, thr, rmin)
    need = jnp.where(active, need, f32(2 ** 30))
    has_ties = jnp.max(jnp.where(jnp.logical_and(active, n_eq > need), 1.0, 0.0)) > 0.5

    for hd in range(N_HEADS):
        g, r = divmod(hd, HEADS_PER_KV)
        blk = q_ref[0, :, (hd // 2) * LANES:(hd // 2 + 1) * LANES]
        keep = low_half if hd % 2 == 0 else jnp.logical_not(low_half)
        qst_ref[g, r * Q_TILE:(r + 1) * Q_TILE, :] = jnp.where(keep, blk, jnp.zeros_like(blk))
    m_ref[...] = jnp.full(m_ref.shape, neg_inf, f32)
    l_ref[...] = jnp.zeros(l_ref.shape, f32)
    acc_ref[...] = jnp.zeros(acc_ref.shape, f32)
    tie_ref[...] = jnp.zeros(tie_ref.shape, f32)

    def attend_tile(j, carry):
        t = sc_ref[j]

        @pl.when(jnp.logical_not(has_ties))
        def _():
            bias_ref[...] = jnp.where(t >= thr, 0.0, neg_inf)

        @pl.when(has_ties)
        def _():
            eq = t == thr
            eqf = jnp.where(eq, 1.0, 0.0)
            rank = tie_ref[...] + _dot(eqf.astype(MXU_DTYPE), tri_ref[...])
            sel = jnp.logical_or(t > thr, jnp.logical_and(eq, rank < need))
            bias_ref[...] = jnp.where(sel, 0.0, neg_inf)
            tie_ref[...] = tie_ref[...] + jnp.sum(eqf, axis=1, keepdims=True)

        bias = bias_ref[...]
        for g in range(N_KV_HEADS):
            s = _dot(qst_ref[g], kt_ref[0, j, g * LANES:(g + 1) * LANES, :])
            s = (s.reshape(HEADS_PER_KV, Q_TILE, K_TILE) + bias[None]).reshape(
                HEADS_PER_KV * Q_TILE, K_TILE)
            m_old = m_ref[g]
            m_new = jnp.maximum(m_old, jnp.max(s, axis=1, keepdims=True))
            m_safe = jnp.where(m_new == neg_inf, 0.0, m_new)
            alpha = jnp.exp(m_old - m_safe)
            p = jnp.exp(s - m_safe)
            l_ref[g] = alpha * l_ref[g] + jnp.sum(p, axis=1, keepdims=True)
            pv = _dot(p.astype(MXU_DTYPE),
                      v_ref[0, pl.ds(pl.multiple_of(j * K_TILE, K_TILE), K_TILE),
                            g * LANES:(g + 1) * LANES])
            acc_ref[g] = alpha * acc_ref[g] + pv
            m_ref[g] = m_new
        return carry

    lax.fori_loop(0, n_kt, attend_tile, 0)

    for g in range(N_KV_HEADS):
        o = acc_ref[g] / l_ref[g]
        for pair in range(HEADS_PER_KV // 2):
            even = o[(2 * pair) * Q_TILE:(2 * pair + 1) * Q_TILE]
            odd = o[(2 * pair + 1) * Q_TILE:(2 * pair + 2) * Q_TILE]
            col = (g * (HEADS_PER_KV // 2) + pair) * LANES
            o_ref[0, :, col:col + LANES] = jnp.where(low_half, even, odd).astype(o_ref.dtype)


def _attn(q, qi, wi, kt, kit, v, tri):
    B, Tp, _ = q.shape
    nkt = Tp // K_TILE
    f32 = jnp.float32
    qrow = lambda width: pl.BlockSpec((1, Q_TILE, width), lambda b, i: (b, i, 0))
    once = pl.Buffered(1)
    return pl.pallas_call(
        _attn_kernel,
        grid=(B, Tp // Q_TILE),
        in_specs=[
            qrow(N_HEADS * HEAD_DIM),
            qrow(IDX_HEADS * IDX_DIM),
            qrow(LANES),
            pl.BlockSpec((1, nkt, N_KV_HEADS * LANES, K_TILE), lambda b, i: (b, 0, 0, 0),
                         pipeline_mode=once),
            pl.BlockSpec((1, nkt, LANES, K_TILE), lambda b, i: (b, 0, 0, 0), pipeline_mode=once),
            pl.BlockSpec((1, Tp, N_KV_HEADS * LANES), lambda b, i: (b, 0, 0), pipeline_mode=once),
            pl.BlockSpec((K_TILE, K_TILE), lambda b, i: (0, 0), pipeline_mode=once),
        ],
        out_specs=qrow(N_HEADS * HEAD_DIM),
        out_shape=jax.ShapeDtypeStruct((B, Tp, N_HEADS * HEAD_DIM), MXU_DTYPE),
        scratch_shapes=[
            pltpu.VMEM((nkt, Q_TILE, K_TILE), f32),
            pltpu.VMEM((IDX_HEADS * Q_TILE, LANES), MXU_DTYPE),
            pltpu.VMEM((IDX_HEADS * Q_TILE, 1), f32),
            pltpu.VMEM((N_KV_HEADS, HEADS_PER_KV * Q_TILE, LANES), MXU_DTYPE),
            pltpu.VMEM((N_KV_HEADS, HEADS_PER_KV * Q_TILE, 1), f32),
            pltpu.VMEM((N_KV_HEADS, HEADS_PER_KV * Q_TILE, 1), f32),
            pltpu.VMEM((N_KV_HEADS, HEADS_PER_KV * Q_TILE, LANES), f32),
            pltpu.VMEM((Q_TILE, K_TILE), f32),
            pltpu.VMEM((Q_TILE, 1), f32),
        ],
        compiler_params=pltpu.CompilerParams(
            dimension_semantics=("arbitrary", "arbitrary"), vmem_limit_bytes=VMEM_LIMIT),
        name="attn",
    )(q, qi, wi, kt, kit, v, tri)


def _mix_kernel(o_ref, u_ref, up_ref, ga_ref, gc_ref, h_ref, wa_ref, wc_ref, wo_ref,
                cw_ref, cb_ref, lg_ref, lb_ref, out_ref, ux_ref):
    f32 = jnp.float32
    i = pl.program_id(1)
    prev = up_ref[0].astype(f32)
    ux_ref[0:CONV_HALO, :] = jnp.where(i > 0, prev, jnp.zeros_like(prev))
    ux_ref[CONV_HALO:CONV_HALO + MIX_TILE, :] = u_ref[0].astype(f32)
    acc = jnp.broadcast_to(cb_ref[...], (MIX_TILE, D_MODEL))
    for j in range(CONV_WIDTH):
        start = CONV_HALO - (CONV_WIDTH - 1) + j
        acc = acc + cw_ref[j:j + 1, :] * ux_ref[start:start + MIX_TILE, :]
    mu = jnp.mean(acc, axis=-1, keepdims=True)
    cen = acc - mu
    var = jnp.mean(cen * cen, axis=-1, keepdims=True)
    y = cen * lax.rsqrt(var + LN_EPS) * lg_ref[...] + lb_ref[...]
    y = y * _sigmoid(y)
    y_conv = _dot(y.astype(MXU_DTYPE), wc_ref[...])
    y_attn = _dot(o_ref[0], wa_ref[...])
    merged = ga_ref[0].astype(f32) * y_attn + gc_ref[0].astype(f32) * y_conv
    out_ref[0] = h_ref[0] + _dot(merged.astype(MXU_DTYPE), wo_ref[...])


def _mix(o, u, ga, gc, h, wa, wc, wo, cw, cb, lg, lb):
    B, Tp, D = h.shape
    row = pl.BlockSpec((1, MIX_TILE, D), lambda b, i: (b, i, 0))
    halo_blocks = MIX_TILE // CONV_HALO
    halo = pl.BlockSpec((1, CONV_HALO, D),
                        lambda b, i: (b, jnp.maximum(i * halo_blocks - 1, 0), 0))
    const = lambda shape: pl.BlockSpec(shape, lambda b, i: (0,) * len(shape),
                                       pipeline_mode=pl.Buffered(1))
    return pl.pallas_call(
        _mix_kernel,
        grid=(B, Tp // MIX_TILE),
        in_specs=[row, row, halo, row, row, row,
                  const((D, D)), const((D, D)), const((D, D)),
                  const((CONV_WIDTH, D)), const((1, D)), const((1, D)), const((1, D))],
        out_specs=row,
        out_shape=jax.ShapeDtypeStruct((B, Tp, D), jnp.float32),
        scratch_shapes=[pltpu.VMEM((CONV_HALO + MIX_TILE, D), jnp.float32)],
        compiler_params=pltpu.CompilerParams(
            dimension_semantics=("arbitrary", "arbitrary"), vmem_limit_bytes=VMEM_LIMIT),
        name="mix",
    )(o, u, u, ga, gc, h, wa, wc, wo, cw, cb, lg, lb)


def _ffn_kernel(h_ref, hp_ref, g_ref, wu_ref, cw_ref, cb_ref, wd_ref, out_ref, xn_ref):
    f32 = jnp.float32
    i = pl.program_id(1)

    def norm(x):
        ms = jnp.mean(x * x, axis=-1, keepdims=True)
        return x * lax.rsqrt(ms + RMS_EPS) * g_ref[...]

    prev = norm(hp_ref[0])
    xn_ref[0:FFN_HALO, :] = jnp.where(i > 0, prev, jnp.zeros_like(prev)).astype(xn_ref.dtype)
    x = h_ref[0]
    xn_ref[FFN_HALO:FFN_HALO + FFN_TILE, :] = norm(x).astype(xn_ref.dtype)
    xn = xn_ref[...]

    def conv(c0):
        hcol = _dot(xn, wu_ref[:, c0:c0 + FFN_CHUNK])
        out = jnp.broadcast_to(cb_ref[:, c0:c0 + FFN_CHUNK], (FFN_TILE, FFN_CHUNK))
        for j in range(FFN_CONV_WIDTH):
            back = FFN_CONV_WIDTH - 1 - j
            shifted = hcol if back == 0 else pltpu.roll(hcol, back, 0)
            out = out + cw_ref[j:j + 1, c0:c0 + FFN_CHUNK] * shifted[FFN_HALO:FFN_HALO + FFN_TILE]
        return out

    acc = jnp.zeros((FFN_TILE, D_MODEL), f32)
    for c0 in range(0, FFN_HIDDEN, FFN_CHUNK):
        gate = conv(c0)
        up = conv(FFN_HIDDEN + c0)
        act = (gate * _sigmoid(gate) * up).astype(MXU_DTYPE)
        acc = acc + _dot(act, wd_ref[c0:c0 + FFN_CHUNK, :])
    out_ref[0] = x + acc


def _ffn(h, g, wu, cw, cb, wd):
    B, Tp, D = h.shape
    row = pl.BlockSpec((1, FFN_TILE, D), lambda b, i: (b, i, 0))
    halo_blocks = FFN_TILE // FFN_HALO
    halo = pl.BlockSpec((1, FFN_HALO, D),
                        lambda b, i: (b, jnp.maximum(i * halo_blocks - 1, 0), 0))
    const = lambda shape: pl.BlockSpec(shape, lambda b, i: (0,) * len(shape),
                                       pipeline_mode=pl.Buffered(1))
    return pl.pallas_call(
        _ffn_kernel,
        grid=(B, Tp // FFN_TILE),
        in_specs=[row, halo, const((1, D)), const((D, 2 * FFN_HIDDEN)),
                  const((FFN_CONV_WIDTH, 2 * FFN_HIDDEN)), const((1, 2 * FFN_HIDDEN)),
                  const((FFN_HIDDEN, D))],
        out_specs=row,
        out_shape=jax.ShapeDtypeStruct((B, Tp, D), jnp.float32),
        scratch_shapes=[pltpu.VMEM((FFN_HALO + FFN_TILE, D), MXU_DTYPE)],
        compiler_params=pltpu.CompilerParams(
            dimension_semantics=("arbitrary", "arbitrary"), vmem_limit_bytes=VMEM_LIMIT),
        name="ffn",
    )(h, h, g, wu, cw, cb, wd)


def _final_norm_kernel(h_ref, g_ref, out_ref):
    x = h_ref[0]
    ms = jnp.mean(x * x, axis=-1, keepdims=True)
    out_ref[0] = x * lax.rsqrt(ms + RMS_EPS) * g_ref[...]


def _final_norm(h, g):
    B, Tp, D = h.shape
    row = pl.BlockSpec((1, ROW_TILE, D), lambda b, i: (b, i, 0))
    return pl.pallas_call(
        _final_norm_kernel,
        grid=(B, Tp // ROW_TILE),
        in_specs=[row, pl.BlockSpec((1, D), lambda b, i: (0, 0))],
        out_specs=row,
        out_shape=jax.ShapeDtypeStruct((B, Tp, D), jnp.float32),
        compiler_params=pltpu.CompilerParams(
            dimension_semantics=("arbitrary", "arbitrary"), vmem_limit_bytes=VMEM_LIMIT),
        name="final_norm",
    )(h, g)


def _rope_tables(Tp):
    half = ROT_DIM // 2
    pos = jnp.arange(Tp, dtype=jnp.float32)
    inv_freq = jnp.power(jnp.float32(ROPE_THETA),
                         -jnp.arange(0, ROT_DIM, 2, dtype=jnp.float32) / ROT_DIM)
    ang = pos[:, None] * inv_freq[None, :]
    cos, sin = jnp.cos(ang), jnp.sin(ang)
    zeros = lambda n: jnp.zeros((Tp, n), jnp.float32)
    cos_h = jnp.concatenate([cos, cos, jnp.ones((Tp, HEAD_DIM - ROT_DIM), jnp.float32)], axis=1)
    sina_h = jnp.concatenate([-sin, zeros(HEAD_DIM - half)], axis=1)
    sinb_h = jnp.concatenate([zeros(half), sin, zeros(HEAD_DIM - ROT_DIM)], axis=1)
    two = lambda t: jnp.concatenate([t, t], axis=1)
    return two(cos_h), two(sina_h), two(sinb_h)


def _pack_w_in(w):
    sizes = (N_HEADS * HEAD_DIM, N_KV_HEADS * HEAD_DIM, N_KV_HEADS * HEAD_DIM,
             IDX_HEADS * IDX_DIM, IDX_DIM, IDX_HEADS, 2 * D_MODEL, D_MODEL, D_MODEL)
    offs = [0]
    for s in sizes:
        offs.append(offs[-1] + s)
    wq, wk, wv, wqi, wki, wwi, wglu, wga, wgc = (w[:, offs[n]:offs[n + 1]] for n in range(9))

    def dup_heads(m):
        n = m.shape[1] // HEAD_DIM
        m = m.reshape(m.shape[0], n, 1, HEAD_DIM)
        return jnp.broadcast_to(m, (m.shape[0], n, 2, HEAD_DIM)).reshape(m.shape[0], n * LANES)

    cols = [wq * (HEAD_DIM ** -0.5), dup_heads(wk), wqi * (IDX_DIM ** -0.5), dup_heads(wki),
            dup_heads(wv), jnp.pad(wwi, ((0, 0), (0, LANES - IDX_HEADS))), wglu, wga, wgc]
    return jnp.concatenate(cols, axis=1).astype(MXU_DTYPE)


def kernel(x, meta_tokens, attn_norm_g, w_in, w_attn_out, conv_dw_w, conv_dw_b, conv_ln_g,
           conv_ln_b, w_conv_out, w_o, ffn_norm_g, w_up, ffn_dw_w, ffn_dw_b, w_down,
           final_norm_g):
    B, S, D = x.shape
    T = N_META + S
    Tp = -(-T // K_TILE) * K_TILE
    bf = MXU_DTYPE
    meta = jnp.broadcast_to(meta_tokens[None].astype(x.dtype), (B, N_META, D))
    h = jnp.concatenate([meta, x, jnp.zeros((B, Tp - T, D), x.dtype)], axis=1)
    cos, sina, sinb = _rope_tables(Tp)
    tri = (lax.broadcasted_iota(jnp.int32, (K_TILE, K_TILE), 0)
           < lax.broadcasted_iota(jnp.int32, (K_TILE, K_TILE), 1)).astype(bf)
    row = lambda v: v.reshape(1, -1)
    for l in range(w_in.shape[0]):
        q, kt, qi, kit, v, wi, u, ga, gc = _proj(h, row(attn_norm_g[l]), _pack_w_in(w_in[l]),
                                                 cos, sina, sinb)
        o = _attn(q, qi, wi, kt, kit, v, tri)
        h = _mix(o, u, ga, gc, h, w_attn_out[l].astype(bf), w_conv_out[l].astype(bf),
                 w_o[l].astype(bf), conv_dw_w[l], row(conv_dw_b[l]), row(conv_ln_g[l]),
                 row(conv_ln_b[l]))
        h = _ffn(h, row(ffn_norm_g[l]), w_up[l].astype(bf), ffn_dw_w[l], row(ffn_dw_b[l]),
                 w_down[l].astype(bf))
    return _final_norm(h, row(final_norm_g))[:, N_META:T]
```

```python
import math

import jax
import jax.numpy as jnp
from jax import lax
from jax.experimental import pallas as pl
from jax.experimental.pallas import tpu as pltpu

D_MODEL = 1024
N_META = 16
N_HEADS = 16
HEAD_DIM = 64
N_KV_HEADS = 4
HEADS_PER_KV = N_HEADS // N_KV_HEADS
N_PAIRS = N_HEADS // 2
IDX_HEADS = 8
IDX_DIM = 64
TOPK_MAX = 256
ROPE_THETA = 500000.0
ROT_DIM = HEAD_DIM // 4
CONV_WIDTH = 31
FFN_HIDDEN = 2816
FFN_CONV_WIDTH = 3
RMS_EPS = 1e-6
LN_EPS = 1e-5

LANES = 128
SUBLANES = 8
Q_TILE = 128
K_TILE = 256
ROW_TILE = 768
MIX_TILE = 384
FFN_TILE = 384
CONV_HALO = 32
FFN_HALO = 16
FFN_CHUNK = 256
N_BISECT = 14
SCORE_UNROLL = 2
SWEEP_UNROLL = 4
VMEM_LIMIT = 56 * 1024 * 1024
MXU_DTYPE = jnp.bfloat16

_C_Q = 0
_C_QI = _C_Q + N_HEADS * HEAD_DIM
_C_K = _C_QI + IDX_HEADS * IDX_DIM
_C_KI = _C_K + N_KV_HEADS * LANES
_C_V = _C_KI + LANES
_C_WI = _C_V + N_KV_HEADS * HEAD_DIM
_C_GLU = _C_WI + LANES
_C_GATE = _C_GLU + 2 * D_MODEL
_C_END = _C_GATE + 2 * D_MODEL


def _sigmoid(x):
    return 1.0 / (1.0 + jnp.exp(-x))


def _dot(a, b):
    return jnp.dot(a, b, preferred_element_type=jnp.float32)


def _proj_kernel(h_ref, g_ref, w_ref, cos_ref, sina_ref, sinb_ref,
                 qt_ref, qit_ref, k_ref, ki_ref, vt_ref, wit_ref, u_ref, ga_ref, gc_ref):
    x = h_ref[0]
    ms = jnp.mean(x * x, axis=-1, keepdims=True)
    xn = (x * lax.rsqrt(ms + RMS_EPS) * g_ref[...]).astype(MXU_DTYPE)
    cos = cos_ref[...]
    sina = sina_ref[...]
    sinb = sinb_ref[...]
    tiles = ROW_TILE // Q_TILE

    def rope(r):
        return (r * cos + pltpu.roll(r, LANES - ROT_DIM // 2, 1) * sina
                + pltpu.roll(r, ROT_DIM // 2, 1) * sinb)

    def proj(c0, width):
        return _dot(xn, w_ref[:, c0:c0 + width])

    def store_transposed(ref, row0, val):
        vt = val.T.astype(ref.dtype)
        for t in range(tiles):
            ref[0, t, row0:row0 + LANES, :] = vt[:, t * Q_TILE:(t + 1) * Q_TILE]

    for c0 in range(_C_Q, _C_QI, 2 * LANES):
        r = proj(c0, 2 * LANES)
        for s in range(2):
            store_transposed(qt_ref, c0 - _C_Q + s * LANES, rope(r[:, s * LANES:(s + 1) * LANES]))
    for c0 in range(_C_QI, _C_K, 2 * LANES):
        r = proj(c0, 2 * LANES)
        for s in range(2):
            store_transposed(qit_ref, c0 - _C_QI + s * LANES, rope(r[:, s * LANES:(s + 1) * LANES]))
    for c0 in range(_C_K, _C_KI, 2 * LANES):
        r = proj(c0, 2 * LANES)
        for s in range(2):
            col = c0 - _C_K + s * LANES
            k_ref[0, :, col:col + LANES] = rope(r[:, s * LANES:(s + 1) * LANES]).astype(k_ref.dtype)
    ki_ref[0] = rope(proj(_C_KI, LANES)).astype(ki_ref.dtype)
    v = proj(_C_V, 2 * LANES)
    for s in range(2):
        vt = v[:, s * LANES:(s + 1) * LANES].T.astype(vt_ref.dtype)
        for t in range(ROW_TILE // K_TILE):
            vt_ref[0, t, s * LANES:(s + 1) * LANES, :] = vt[:, t * K_TILE:(t + 1) * K_TILE]
    wit = (proj(_C_WI, LANES) * (IDX_HEADS ** -0.5)).T
    for t in range(tiles):
        wit_ref[0, t] = wit[0:IDX_HEADS, t * Q_TILE:(t + 1) * Q_TILE]
    for c0 in range(0, D_MODEL, 2 * LANES):
        a = proj(_C_GLU + c0, 2 * LANES)
        b = proj(_C_GLU + D_MODEL + c0, 2 * LANES)
        u_ref[0, :, c0:c0 + 2 * LANES] = (a * _sigmoid(b)).astype(u_ref.dtype)
    for c0 in range(0, D_MODEL, 2 * LANES):
        ga_ref[0, :, c0:c0 + 2 * LANES] = _sigmoid(proj(_C_GATE + c0, 2 * LANES)).astype(ga_ref.dtype)
        gc_ref[0, :, c0:c0 + 2 * LANES] = _sigmoid(
            proj(_C_GATE + D_MODEL + c0, 2 * LANES)).astype(gc_ref.dtype)


def _proj(h, g, w, cos, sina, sinb):
    B, Tp, D = h.shape
    nt = Tp // ROW_TILE
    nq = Tp // Q_TILE
    nk = Tp // K_TILE
    qpt = ROW_TILE // Q_TILE
    kpt = ROW_TILE // K_TILE
    bf = MXU_DTYPE
    row = lambda width: pl.BlockSpec((1, ROW_TILE, width), lambda b, i: (b, i, 0))
    tab = pl.BlockSpec((ROW_TILE, LANES), lambda b, i: (i, 0))
    per_qtile = lambda rows: pl.BlockSpec((1, qpt, rows, Q_TILE), lambda b, i: (b, i, 0, 0))
    out_shape = (
        jax.ShapeDtypeStruct((B, nq, N_HEADS * HEAD_DIM, Q_TILE), bf),
        jax.ShapeDtypeStruct((B, nq, IDX_HEADS * IDX_DIM, Q_TILE), bf),
        jax.ShapeDtypeStruct((B, Tp, N_KV_HEADS * LANES), bf),
        jax.ShapeDtypeStruct((B, Tp, LANES), bf),
        jax.ShapeDtypeStruct((B, nk, N_KV_HEADS * HEAD_DIM, K_TILE), bf),
        jax.ShapeDtypeStruct((B, nq, IDX_HEADS, Q_TILE), jnp.float32),
        jax.ShapeDtypeStruct((B, Tp, D), bf),
        jax.ShapeDtypeStruct((B, Tp, D), bf),
        jax.ShapeDtypeStruct((B, Tp, D), bf),
    )
    out_specs = (
        per_qtile(N_HEADS * HEAD_DIM),
        per_qtile(IDX_HEADS * IDX_DIM),
        row(N_KV_HEADS * LANES),
        row(LANES),
        pl.BlockSpec((1, kpt, N_KV_HEADS * HEAD_DIM, K_TILE), lambda b, i: (b, i, 0, 0)),
        per_qtile(IDX_HEADS),
        row(D), row(D), row(D),
    )
    return pl.pallas_call(
        _proj_kernel,
        grid=(B, nt),
        in_specs=[
            row(D),
            pl.BlockSpec((1, D), lambda b, i: (0, 0)),
            pl.BlockSpec((D, _C_END), lambda b, i: (0, 0), pipeline_mode=pl.Buffered(1)),
            tab, tab, tab,
        ],
        out_specs=out_specs,
        out_shape=out_shape,
        compiler_params=pltpu.CompilerParams(
            dimension_semantics=("arbitrary", "arbitrary"), vmem_limit_bytes=VMEM_LIMIT),
        name="proj",
    )(h, g, w, cos, sina, sinb)


def _fold_sublanes(x, op):
    parts = [x[r:r + SUBLANES] for r in range(0, x.shape[0], SUBLANES)]
    while len(parts) > 1:
        parts = [op(parts[n], parts[n + 1]) if n + 1 < len(parts) else parts[n]
                 for n in range(0, len(parts), 2)]
    return parts[0]


def _attn_kernel(qt_ref, qit_ref, wit_ref, k_ref, ki_ref, vt_ref, tri_ref, o_ref,
                 sc_ref, qi_s, wb_s, qp_s, m_s, l_s, acc_s, bias_s, tie_s, ot_s,
                 s_s, mt_s, p_s, al_s):
    f32 = jnp.float32
    i = pl.program_id(1)
    n_kt = (i * Q_TILE + Q_TILE + K_TILE - 1) // K_TILE
    qpos = i * Q_TILE + lax.broadcasted_iota(jnp.int32, (1, Q_TILE), 1)
    neg_inf = f32(-jnp.inf)
    pos_inf = f32(jnp.inf)
    lane_vec = lambda v: jnp.full((1, Q_TILE), v, f32)

    qi_s[...] = jnp.zeros(qi_s.shape, qi_s.dtype)
    for hd in range(IDX_HEADS):
        qi_s[0:IDX_DIM, hd * Q_TILE:(hd + 1) * Q_TILE] = qit_ref[0, 0, hd * IDX_DIM:(hd + 1) * IDX_DIM, :]
        wb_s[:, hd * Q_TILE:(hd + 1) * Q_TILE] = wit_ref[0, 0, hd:hd + 1, :]

    n_score = (n_kt + SCORE_UNROLL - 1) // SCORE_UNROLL
    n_sweep = (n_kt + SWEEP_UNROLL - 1) // SWEEP_UNROLL
    last_row0 = k_ref.shape[1] - K_TILE

    def score_tile(j, carry):
        rmax, rmin = carry
        row0 = pl.multiple_of(j * K_TILE, K_TILE)
        read0 = pl.multiple_of(jnp.minimum(row0, last_row0), K_TILE)
        lg = _dot(ki_ref[0, pl.ds(read0, K_TILE), :], qi_s[...])
        r = jnp.maximum(lg, 0.0) * wb_s[...]
        s = r[:, 0:Q_TILE]
        for hd in range(1, IDX_HEADS):
            s = s + r[:, hd * Q_TILE:(hd + 1) * Q_TILE]
        kpos = row0 + lax.broadcasted_iota(jnp.int32, (K_TILE, 1), 0)
        causal = kpos <= qpos
        masked = jnp.where(causal, s, neg_inf)
        sc_ref[j] = masked
        rmax = jnp.maximum(rmax, _fold_sublanes(masked, jnp.maximum))
        rmin = jnp.minimum(rmin, _fold_sublanes(jnp.where(causal, s, pos_inf), jnp.minimum))
        return rmax, rmin

    def score_group(jj, carry):
        for u in range(SCORE_UNROLL):
            carry = score_tile(jj * SCORE_UNROLL + u, carry)
        return carry

    rmax, rmin = lax.fori_loop(
        0, n_score, score_group,
        (jnp.full((SUBLANES, Q_TILE), neg_inf, f32), jnp.full((SUBLANES, Q_TILE), pos_inf, f32)))
    rmax = jnp.max(rmax, axis=0, keepdims=True)
    rmin = jnp.min(rmin, axis=0, keepdims=True)

    def fill_tile(j, carry):
        sc_ref[j] = jnp.full((K_TILE, Q_TILE), neg_inf, f32)
        return carry

    lax.fori_loop(n_score * SCORE_UNROLL, n_sweep * SWEEP_UNROLL, fill_tile, 0)

    def count_where(pred_fn):
        def body(jj, acc):
            for u in range(SWEEP_UNROLL):
                t = sc_ref[jj * SWEEP_UNROLL + u]
                acc = acc + _fold_sublanes(jnp.where(pred_fn(t), 1.0, 0.0), jnp.add)
            return acc
        acc = lax.fori_loop(0, n_sweep, body, jnp.zeros((SUBLANES, Q_TILE), f32))
        return jnp.sum(acc, axis=0, keepdims=True)

    def max_below(bound):
        def body(jj, acc):
            for u in range(SWEEP_UNROLL):
                t = sc_ref[jj * SWEEP_UNROLL + u]
                acc = jnp.maximum(
                    acc, _fold_sublanes(jnp.where(t < bound, t, neg_inf), jnp.maximum))
            return acc
        acc = lax.fori_loop(0, n_sweep, body, jnp.full((SUBLANES, Q_TILE), neg_inf, f32))
        return jnp.max(acc, axis=0, keepdims=True)

    k_sel = f32(TOPK_MAX)
    active = qpos >= TOPK_MAX

    def bisect(_, carry):
        lo, hi, c_hi, strict = carry
        mid = lo + 0.5 * (hi - lo)
        c = count_where(lambda t: t >= mid)
        ge = c >= k_sel
        return (jnp.where(ge, mid, lo), jnp.where(ge, hi, mid),
                jnp.where(ge, c_hi, c), jnp.where(ge, strict, 1.0))

    lo, hi, c_hi, strict = lax.fori_loop(
        0, N_BISECT, bisect, (rmin, rmax, lane_vec(0.0), lane_vec(0.0)))

    def walk_cond(carry):
        return jnp.min(carry[0]) < 0.5

    def walk_body(carry):
        done, bound, cnt, thr, need, n_eq = carry
        x = max_below(bound)
        e = count_where(lambda t: t == x)
        finished = jnp.logical_or(cnt + e >= k_sel, x == neg_inf)
        newly = jnp.logical_and(done < 0.5, finished)
        thr = jnp.where(newly, x, thr)
        need = jnp.where(newly, k_sel - cnt, need)
        n_eq = jnp.where(newly, e, n_eq)
        done = jnp.where(newly, 1.0, done)
        still = done < 0.5
        return (done, jnp.where(still, x, bound), jnp.where(still, cnt + e, cnt), thr, need, n_eq)

    is_strict = strict > 0.5
    _, _, _, thr, need, n_eq = lax.while_loop(
        walk_cond, walk_body,
        (jnp.where(active, 0.0, 1.0), jnp.where(is_strict, hi, pos_inf),
         jnp.where(is_strict, c_hi, 0.0), lane_vec(0.0), lane_vec(0.0), lane_vec(0.0)))
    thr = jnp.where(active, thr, rmin)
    need = jnp.where(active, need, f32(2 ** 30))
    has_ties = jnp.max(jnp.where(jnp.logical_and(active, n_eq > need), 1.0, 0.0)) > 0.5

    qp_s[...] = jnp.zeros(qp_s.shape, qp_s.dtype)
    for p in range(N_PAIRS):
        qp_s[p, 0:HEAD_DIM, 0:Q_TILE] = qt_ref[0, 0, (2 * p) * HEAD_DIM:(2 * p + 1) * HEAD_DIM, :]
        qp_s[p, HEAD_DIM:LANES, Q_TILE:2 * Q_TILE] = qt_ref[
            0, 0, (2 * p + 1) * HEAD_DIM:(2 * p + 2) * HEAD_DIM, :]
    m_s[...] = jnp.full(m_s.shape, neg_inf, f32)
    l_s[...] = jnp.zeros(l_s.shape, f32)
    acc_s[...] = jnp.zeros(acc_s.shape, f32)
    tie_s[...] = jnp.zeros(tie_s.shape, f32)

    def attend_tile(j, carry):
        row0 = pl.multiple_of(j * K_TILE, K_TILE)
        t = sc_ref[j]

        @pl.when(jnp.logical_not(has_ties))
        def _():
            bias_s[...] = jnp.where(t >= thr, 0.0, neg_inf)

        @pl.when(has_ties)
        def _():
            eq = t == thr
            eqf = jnp.where(eq, 1.0, 0.0)
            rank = tie_s[...] + _dot(tri_ref[...], eqf.astype(MXU_DTYPE))
            sel = jnp.logical_or(t > thr, jnp.logical_and(eq, rank < need))
            bias_s[...] = jnp.where(sel, 0.0, neg_inf)
            tie_s[...] = tie_s[...] + jnp.sum(eqf, axis=0, keepdims=True)

        for p in range(N_PAIRS):
            g = (2 * p) // HEADS_PER_KV
            s = _dot(k_ref[0, pl.ds(row0, K_TILE), g * LANES:(g + 1) * LANES], qp_s[p])
            for half in range(2):
                sh = s[:, half * Q_TILE:(half + 1) * Q_TILE] + bias_s[...]
                s_s[p, :, half * Q_TILE:(half + 1) * Q_TILE] = sh
                mt_s[p, :, half * Q_TILE:(half + 1) * Q_TILE] = jnp.max(
                    _fold_sublanes(sh, jnp.maximum), axis=0, keepdims=True)
        for p in range(N_PAIRS):
            m_old = m_s[p]
            m_new = jnp.maximum(m_old, mt_s[p])
            m_safe = jnp.where(m_new == neg_inf, 0.0, m_new)
            alpha = jnp.exp2(m_old - m_safe)
            pm = jnp.exp2(s_s[p] - m_safe)
            l_s[p] = alpha * l_s[p] + jnp.sum(_fold_sublanes(pm, jnp.add), axis=0, keepdims=True)
            p_s[p] = pm.astype(MXU_DTYPE)
            al_s[p] = alpha
            m_s[p] = m_new
        for p in range(N_PAIRS):
            g = (2 * p) // HEADS_PER_KV
            pv = _dot(vt_ref[0, j, g * HEAD_DIM:(g + 1) * HEAD_DIM, :], p_s[p])
            acc_s[p] = al_s[p] * acc_s[p] + pv
        return carry

    lax.fori_loop(0, n_kt, attend_tile, 0)

    for p in range(N_PAIRS):
        o = acc_s[p] / l_s[p]
        ot_s[(2 * p) * HEAD_DIM:(2 * p + 1) * HEAD_DIM, :] = o[:, 0:Q_TILE]
        ot_s[(2 * p + 1) * HEAD_DIM:(2 * p + 2) * HEAD_DIM, :] = o[:, Q_TILE:2 * Q_TILE]
    o_ref[0] = ot_s[...].T.astype(o_ref.dtype)


def _attn(qt, qit, wit, k, ki, vt, tri):
    B, nq, _, _ = qt.shape
    Tp = nq * Q_TILE
    nkt = Tp // K_TILE
    f32 = jnp.float32
    per_q = lambda rows: pl.BlockSpec((1, 1, rows, Q_TILE), lambda b, i: (b, i, 0, 0))
    once = pl.Buffered(1)
    return pl.pallas_call(
        _attn_kernel,
        grid=(B, nq),
        in_specs=[
            per_q(N_HEADS * HEAD_DIM),
            per_q(IDX_HEADS * IDX_DIM),
            per_q(IDX_HEADS),
            pl.BlockSpec((1, Tp, N_KV_HEADS * LANES), lambda b, i: (b, 0, 0), pipeline_mode=once),
            pl.BlockSpec((1, Tp, LANES), lambda b, i: (b, 0, 0), pipeline_mode=once),
            pl.BlockSpec((1, nkt, N_KV_HEADS * HEAD_DIM, K_TILE), lambda b, i: (b, 0, 0, 0),
                         pipeline_mode=once),
            pl.BlockSpec((K_TILE, K_TILE), lambda b, i: (0, 0), pipeline_mode=once),
        ],
        out_specs=pl.BlockSpec((1, Q_TILE, N_HEADS * HEAD_DIM), lambda b, i: (b, i, 0)),
        out_shape=jax.ShapeDtypeStruct((B, Tp, N_HEADS * HEAD_DIM), MXU_DTYPE),
        scratch_shapes=[
            pltpu.VMEM((-(-nkt // SWEEP_UNROLL) * SWEEP_UNROLL, K_TILE, Q_TILE), f32),
            pltpu.VMEM((LANES, IDX_HEADS * Q_TILE), MXU_DTYPE),
            pltpu.VMEM((1, IDX_HEADS * Q_TILE), f32),
            pltpu.VMEM((N_PAIRS, LANES, 2 * Q_TILE), MXU_DTYPE),
            pltpu.VMEM((N_PAIRS, 1, 2 * Q_TILE), f32),
            pltpu.VMEM((N_PAIRS, 1, 2 * Q_TILE), f32),
            pltpu.VMEM((N_PAIRS, HEAD_DIM, 2 * Q_TILE), f32),
            pltpu.VMEM((K_TILE, Q_TILE), f32),
            pltpu.VMEM((1, Q_TILE), f32),
            pltpu.VMEM((N_HEADS * HEAD_DIM, Q_TILE), f32),
            pltpu.VMEM((N_PAIRS, K_TILE, 2 * Q_TILE), f32),
            pltpu.VMEM((N_PAIRS, 1, 2 * Q_TILE), f32),
            pltpu.VMEM((N_PAIRS, K_TILE, 2 * Q_TILE), MXU_DTYPE),
            pltpu.VMEM((N_PAIRS, 1, 2 * Q_TILE), f32),
        ],
        compiler_params=pltpu.CompilerParams(
            dimension_semantics=("arbitrary", "arbitrary"), vmem_limit_bytes=VMEM_LIMIT),
        name="attn",
    )(qt, qit, wit, k, ki, vt, tri)


def _mix_kernel(o_ref, u_ref, up_ref, ga_ref, gc_ref, h_ref, wa_ref, wc_ref, wo_ref,
                cw_ref, cb_ref, lg_ref, lb_ref, out_ref, ux_ref):
    f32 = jnp.float32
    i = pl.program_id(1)
    prev = up_ref[0].astype(f32)
    ux_ref[0:CONV_HALO, :] = jnp.where(i > 0, prev, jnp.zeros_like(prev))
    ux_ref[CONV_HALO:CONV_HALO + MIX_TILE, :] = u_ref[0].astype(f32)
    acc = jnp.broadcast_to(cb_ref[...], (MIX_TILE, D_MODEL))
    for j in range(CONV_WIDTH):
        start = CONV_HALO - (CONV_WIDTH - 1) + j
        acc = acc + cw_ref[j:j + 1, :] * ux_ref[start:start + MIX_TILE, :]
    mu = jnp.mean(acc, axis=-1, keepdims=True)
    cen = acc - mu
    var = jnp.mean(cen * cen, axis=-1, keepdims=True)
    y = cen * lax.rsqrt(var + LN_EPS) * lg_ref[...] + lb_ref[...]
    y = y * _sigmoid(y)
    y_conv = _dot(y.astype(MXU_DTYPE), wc_ref[...])
    y_attn = _dot(o_ref[0], wa_ref[...])
    merged = ga_ref[0].astype(f32) * y_attn + gc_ref[0].astype(f32) * y_conv
    out_ref[0] = h_ref[0] + _dot(merged.astype(MXU_DTYPE), wo_ref[...])


def _mix(o, u, ga, gc, h, wa, wc, wo, cw, cb, lg, lb):
    B, Tp, D = h.shape
    row = pl.BlockSpec((1, MIX_TILE, D), lambda b, i: (b, i, 0))
    halo_blocks = MIX_TILE // CONV_HALO
    halo = pl.BlockSpec((1, CONV_HALO, D),
                        lambda b, i: (b, jnp.maximum(i * halo_blocks - 1, 0), 0))
    const = lambda shape: pl.BlockSpec(shape, lambda b, i: (0,) * len(shape),
                                       pipeline_mode=pl.Buffered(1))
    return pl.pallas_call(
        _mix_kernel,
        grid=(B, Tp // MIX_TILE),
        in_specs=[row, row, halo, row, row, row,
                  const((D, D)), const((D, D)), const((D, D)),
                  const((CONV_WIDTH, D)), const((1, D)), const((1, D)), const((1, D))],
        out_specs=row,
        out_shape=jax.ShapeDtypeStruct((B, Tp, D), jnp.float32),
        scratch_shapes=[pltpu.VMEM((CONV_HALO + MIX_TILE, D), jnp.float32)],
        compiler_params=pltpu.CompilerParams(
            dimension_semantics=("arbitrary", "arbitrary"), vmem_limit_bytes=VMEM_LIMIT),
        name="mix",
    )(o, u, u, ga, gc, h, wa, wc, wo, cw, cb, lg, lb)


def _ffn_kernel(h_ref, hp_ref, g_ref, wu_ref, cw_ref, cb_ref, wd_ref, out_ref, xn_ref):
    f32 = jnp.float32
    i = pl.program_id(1)

    def norm(x):
        ms = jnp.mean(x * x, axis=-1, keepdims=True)
        return x * lax.rsqrt(ms + RMS_EPS) * g_ref[...]

    prev = norm(hp_ref[0])
    xn_ref[0:FFN_HALO, :] = jnp.where(i > 0, prev, jnp.zeros_like(prev)).astype(xn_ref.dtype)
    x = h_ref[0]
    xn_ref[FFN_HALO:FFN_HALO + FFN_TILE, :] = norm(x).astype(xn_ref.dtype)
    xn = xn_ref[...]

    def conv(c0):
        hcol = _dot(xn, wu_ref[:, c0:c0 + FFN_CHUNK])
        out = jnp.broadcast_to(cb_ref[:, c0:c0 + FFN_CHUNK], (FFN_TILE, FFN_CHUNK))
        for j in range(FFN_CONV_WIDTH):
            back = FFN_CONV_WIDTH - 1 - j
            shifted = hcol if back == 0 else pltpu.roll(hcol, back, 0)
            out = out + cw_ref[j:j + 1, c0:c0 + FFN_CHUNK] * shifted[FFN_HALO:FFN_HALO + FFN_TILE]
        return out

    acc = jnp.zeros((FFN_TILE, D_MODEL), f32)
    for c0 in range(0, FFN_HIDDEN, FFN_CHUNK):
        gate = conv(c0)
        up = conv(FFN_HIDDEN + c0)
        act = (gate * _sigmoid(gate) * up).astype(MXU_DTYPE)
        acc = acc + _dot(act, wd_ref[c0:c0 + FFN_CHUNK, :])
    out_ref[0] = x + acc


def _ffn(h, g, wu, cw, cb, wd):
    B, Tp, D = h.shape
    row = pl.BlockSpec((1, FFN_TILE, D), lambda b, i: (b, i, 0))
    halo_blocks = FFN_TILE // FFN_HALO
    halo = pl.BlockSpec((1, FFN_HALO, D),
                        lambda b, i: (b, jnp.maximum(i * halo_blocks - 1, 0), 0))
    const = lambda shape: pl.BlockSpec(shape, lambda b, i: (0,) * len(shape),
                                       pipeline_mode=pl.Buffered(1))
    return pl.pallas_call(
        _ffn_kernel,
        grid=(B, Tp // FFN_TILE),
        in_specs=[row, halo, const((1, D)), const((D, 2 * FFN_HIDDEN)),
                  const((FFN_CONV_WIDTH, 2 * FFN_HIDDEN)), const((1, 2 * FFN_HIDDEN)),
                  const((FFN_HIDDEN, D))],
        out_specs=row,
        out_shape=jax.ShapeDtypeStruct((B, Tp, D), jnp.float32),
        scratch_shapes=[pltpu.VMEM((FFN_HALO + FFN_TILE, D), MXU_DTYPE)],
        compiler_params=pltpu.CompilerParams(
            dimension_semantics=("arbitrary", "arbitrary"), vmem_limit_bytes=VMEM_LIMIT),
        name="ffn",
    )(h, h, g, wu, cw, cb, wd)


def _final_norm_kernel(h_ref, g_ref, out_ref):
    x = h_ref[0]
    ms = jnp.mean(x * x, axis=-1, keepdims=True)
    out_ref[0] = x * lax.rsqrt(ms + RMS_EPS) * g_ref[...]


def _final_norm(h, g):
    B, Tp, D = h.shape
    row = pl.BlockSpec((1, ROW_TILE, D), lambda b, i: (b, i, 0))
    return pl.pallas_call(
        _final_norm_kernel,
        grid=(B, Tp // ROW_TILE),
        in_specs=[row, pl.BlockSpec((1, D), lambda b, i: (0, 0))],
        out_specs=row,
        out_shape=jax.ShapeDtypeStruct((B, Tp, D), jnp.float32),
        compiler_params=pltpu.CompilerParams(
            dimension_semantics=("arbitrary", "arbitrary"), vmem_limit_bytes=VMEM_LIMIT),
        name="final_norm",
    )(h, g)


def _rope_tables(Tp):
    half = ROT_DIM // 2
    pos = jnp.arange(Tp, dtype=jnp.float32)
    inv_freq = jnp.power(jnp.float32(ROPE_THETA),
                         -jnp.arange(0, ROT_DIM, 2, dtype=jnp.float32) / ROT_DIM)
    ang = pos[:, None] * inv_freq[None, :]
    cos, sin = jnp.cos(ang), jnp.sin(ang)
    zeros = lambda n: jnp.zeros((Tp, n), jnp.float32)
    cos_h = jnp.concatenate([cos, cos, jnp.ones((Tp, HEAD_DIM - ROT_DIM), jnp.float32)], axis=1)
    sina_h = jnp.concatenate([-sin, zeros(HEAD_DIM - half)], axis=1)
    sinb_h = jnp.concatenate([zeros(half), sin, zeros(HEAD_DIM - ROT_DIM)], axis=1)
    two = lambda t: jnp.concatenate([t, t], axis=1)
    return two(cos_h), two(sina_h), two(sinb_h)


def _pack_w_in(w):
    sizes = (N_HEADS * HEAD_DIM, N_KV_HEADS * HEAD_DIM, N_KV_HEADS * HEAD_DIM,
             IDX_HEADS * IDX_DIM, IDX_DIM, IDX_HEADS, 2 * D_MODEL, D_MODEL, D_MODEL)
    offs = [0]
    for s in sizes:
        offs.append(offs[-1] + s)
    wq, wk, wv, wqi, wki, wwi, wglu, wga, wgc = (w[:, offs[n]:offs[n + 1]] for n in range(9))

    def dup_heads(m):
        n = m.shape[1] // HEAD_DIM
        m = m.reshape(m.shape[0], n, 1, HEAD_DIM)
        return jnp.broadcast_to(m, (m.shape[0], n, 2, HEAD_DIM)).reshape(m.shape[0], n * LANES)

    pad_lanes = lambda m: jnp.pad(m, ((0, 0), (0, LANES - m.shape[1])))
    q_scale = (HEAD_DIM ** -0.5) * math.log2(math.e)
    cols = [wq * q_scale, wqi * (IDX_DIM ** -0.5), dup_heads(wk), pad_lanes(wki), wv,
            pad_lanes(wwi), wglu, wga, wgc]
    return jnp.concatenate(cols, axis=1).astype(MXU_DTYPE)


def kernel(x, meta_tokens, attn_norm_g, w_in, w_attn_out, conv_dw_w, conv_dw_b, conv_ln_g,
           conv_ln_b, w_conv_out, w_o, ffn_norm_g, w_up, ffn_dw_w, ffn_dw_b, w_down,
           final_norm_g):
    B, S, D = x.shape
    T = N_META + S
    Tp = -(-T // ROW_TILE) * ROW_TILE
    bf = MXU_DTYPE
    meta = jnp.broadcast_to(meta_tokens[None].astype(x.dtype), (B, N_META, D))
    h = jnp.concatenate([meta, x, jnp.zeros((B, Tp - T, D), x.dtype)], axis=1)
    cos, sina, sinb = _rope_tables(Tp)
    tri = (lax.broadcasted_iota(jnp.int32, (K_TILE, K_TILE), 1)
           < lax.broadcasted_iota(jnp.int32, (K_TILE, K_TILE), 0)).astype(bf)
    row = lambda v: v.reshape(1, -1)
    for l in range(w_in.shape[0]):
        qt, qit, k, ki, vt, wit, u, ga, gc = _proj(h, row(attn_norm_g[l]), _pack_w_in(w_in[l]),
                                                   cos, sina, sinb)
        o = _attn(qt, qit, wit, k, ki, vt, tri)
        h = _mix(o, u, ga, gc, h, w_attn_out[l].astype(bf), w_conv_out[l].astype(bf),
                 w_o[l].astype(bf), conv_dw_w[l], row(conv_dw_b[l]), row(conv_ln_g[l]),
                 row(conv_ln_b[l]))
        h = _ffn(h, row(ffn_norm_g[l]), w_up[l].astype(bf), ffn_dw_w[l], row(ffn_dw_b[l]),
                 w_down[l].astype(bf))
    return _final_norm(h, row(final_norm_g))[:, N_META:T]
```

```python
import math

import jax
import jax.numpy as jnp
from jax import lax
from jax.experimental import pallas as pl
from jax.experimental.pallas import tpu as pltpu

D_MODEL = 1024
N_META = 16
N_HEADS = 16
HEAD_DIM = 64
N_KV_HEADS = 4
HEADS_PER_KV = N_HEADS // N_KV_HEADS
N_PAIRS = N_HEADS // 2
IDX_HEADS = 8
IDX_DIM = 64
TOPK_MAX = 256
ROPE_THETA = 500000.0
ROT_DIM = HEAD_DIM // 4
CONV_WIDTH = 31
FFN_HIDDEN = 2816
FFN_CONV_WIDTH = 3
RMS_EPS = 1e-6
LN_EPS = 1e-5

LANES = 128
SUBLANES = 8
Q_TILE = 128
K_TILE = 256
ROW_TILE = 768
MIX_TILE = 384
FFN_TILE = 384
CONV_HALO = 32
CONV_COLS = 256
FFN_HALO = 16
FFN_CHUNK = 256
N_BISECT = 14
SCORE_UNROLL = 2
SWEEP_UNROLL = 4
EXP_ROWS = 32
DENOM_ROWS = 16
VMEM_LIMIT = 56 * 1024 * 1024
MXU_DTYPE = jnp.bfloat16

_C_Q = 0
_C_QI = _C_Q + N_HEADS * HEAD_DIM
_C_K = _C_QI + IDX_HEADS * IDX_DIM
_C_KI = _C_K + N_KV_HEADS * LANES
_C_V = _C_KI + LANES
_C_WI = _C_V + N_KV_HEADS * HEAD_DIM
_C_GLU = _C_WI + LANES
_C_GATE = _C_GLU + 2 * D_MODEL
_C_END = _C_GATE + 2 * D_MODEL


def _sigmoid(x):
    return 1.0 / (1.0 + jnp.exp(-x))


def _dot(a, b):
    return jnp.dot(a, b, preferred_element_type=jnp.float32)


def _proj_kernel(h_ref, g_ref, w_ref, cos_ref, sina_ref, sinb_ref,
                 qt_ref, qit_ref, k_ref, ki_ref, vt_ref, wit_ref, u_ref, ga_ref, gc_ref):
    x = h_ref[0]
    ms = jnp.mean(x * x, axis=-1, keepdims=True)
    xn = (x * lax.rsqrt(ms + RMS_EPS) * g_ref[...]).astype(MXU_DTYPE)
    cos = cos_ref[...]
    sina = sina_ref[...]
    sinb = sinb_ref[...]
    tiles = ROW_TILE // Q_TILE

    def rope(r):
        return (r * cos + pltpu.roll(r, LANES - ROT_DIM // 2, 1) * sina
                + pltpu.roll(r, ROT_DIM // 2, 1) * sinb)

    def proj(c0, width):
        return _dot(xn, w_ref[:, c0:c0 + width])

    def store_transposed(ref, row0, val):
        vt = val.T.astype(ref.dtype)
        for t in range(tiles):
            ref[0, t, row0:row0 + LANES, :] = vt[:, t * Q_TILE:(t + 1) * Q_TILE]

    for c0 in range(_C_Q, _C_QI, 2 * LANES):
        r = proj(c0, 2 * LANES)
        for s in range(2):
            store_transposed(qt_ref, c0 - _C_Q + s * LANES, rope(r[:, s * LANES:(s + 1) * LANES]))
    for c0 in range(_C_QI, _C_K, 2 * LANES):
        r = proj(c0, 2 * LANES)
        for s in range(2):
            store_transposed(qit_ref, c0 - _C_QI + s * LANES, rope(r[:, s * LANES:(s + 1) * LANES]))
    for c0 in range(_C_K, _C_KI, 2 * LANES):
        r = proj(c0, 2 * LANES)
        for s in range(2):
            col = c0 - _C_K + s * LANES
            k_ref[0, :, col:col + LANES] = rope(r[:, s * LANES:(s + 1) * LANES]).astype(k_ref.dtype)
    ki_ref[0] = rope(proj(_C_KI, LANES)).astype(ki_ref.dtype)
    v = proj(_C_V, 2 * LANES)
    for s in range(2):
        vt = v[:, s * LANES:(s + 1) * LANES].T.astype(vt_ref.dtype)
        for t in range(ROW_TILE // K_TILE):
            vt_ref[0, t, s * LANES:(s + 1) * LANES, :] = vt[:, t * K_TILE:(t + 1) * K_TILE]
    wit = (proj(_C_WI, LANES) * (IDX_HEADS ** -0.5)).T
    for t in range(tiles):
        wit_ref[0, t] = wit[0:IDX_HEADS, t * Q_TILE:(t + 1) * Q_TILE]
    for c0 in range(0, D_MODEL, 2 * LANES):
        a = proj(_C_GLU + c0, 2 * LANES)
        b = proj(_C_GLU + D_MODEL + c0, 2 * LANES)
        u_ref[0, :, c0:c0 + 2 * LANES] = (a * _sigmoid(b)).astype(u_ref.dtype)
    for c0 in range(0, D_MODEL, 2 * LANES):
        ga_ref[0, :, c0:c0 + 2 * LANES] = _sigmoid(proj(_C_GATE + c0, 2 * LANES)).astype(ga_ref.dtype)
        gc_ref[0, :, c0:c0 + 2 * LANES] = _sigmoid(
            proj(_C_GATE + D_MODEL + c0, 2 * LANES)).astype(gc_ref.dtype)


def _proj(h, g, w, cos, sina, sinb):
    B, Tp, D = h.shape
    nt = Tp // ROW_TILE
    nq = Tp // Q_TILE
    nk = Tp // K_TILE
    qpt = ROW_TILE // Q_TILE
    kpt = ROW_TILE // K_TILE
    bf = MXU_DTYPE
    row = lambda width: pl.BlockSpec((1, ROW_TILE, width), lambda b, i: (b, i, 0))
    tab = pl.BlockSpec((ROW_TILE, LANES), lambda b, i: (i, 0))
    per_qtile = lambda rows: pl.BlockSpec((1, qpt, rows, Q_TILE), lambda b, i: (b, i, 0, 0))
    out_shape = (
        jax.ShapeDtypeStruct((B, nq, N_HEADS * HEAD_DIM, Q_TILE), bf),
        jax.ShapeDtypeStruct((B, nq, IDX_HEADS * IDX_DIM, Q_TILE), bf),
        jax.ShapeDtypeStruct((B, Tp, N_KV_HEADS * LANES), bf),
        jax.ShapeDtypeStruct((B, Tp, LANES), bf),
        jax.ShapeDtypeStruct((B, nk, N_KV_HEADS * HEAD_DIM, K_TILE), bf),
        jax.ShapeDtypeStruct((B, nq, IDX_HEADS, Q_TILE), jnp.float32),
        jax.ShapeDtypeStruct((B, Tp, D), bf),
        jax.ShapeDtypeStruct((B, Tp, D), bf),
        jax.ShapeDtypeStruct((B, Tp, D), bf),
    )
    out_specs = (
        per_qtile(N_HEADS * HEAD_DIM),
        per_qtile(IDX_HEADS * IDX_DIM),
        row(N_KV_HEADS * LANES),
        row(LANES),
        pl.BlockSpec((1, kpt, N_KV_HEADS * HEAD_DIM, K_TILE), lambda b, i: (b, i, 0, 0)),
        per_qtile(IDX_HEADS),
        row(D), row(D), row(D),
    )
    return pl.pallas_call(
        _proj_kernel,
        grid=(B, nt),
        in_specs=[
            row(D),
            pl.BlockSpec((1, D), lambda b, i: (0, 0)),
            pl.BlockSpec((D, _C_END), lambda b, i: (0, 0), pipeline_mode=pl.Buffered(1)),
            tab, tab, tab,
        ],
        out_specs=out_specs,
        out_shape=out_shape,
        compiler_params=pltpu.CompilerParams(
            dimension_semantics=("arbitrary", "arbitrary"), vmem_limit_bytes=VMEM_LIMIT),
        name="proj",
    )(h, g, w, cos, sina, sinb)


def _fold_sublanes(x, op):
    parts = [x[r:r + SUBLANES] for r in range(0, x.shape[0], SUBLANES)]
    while len(parts) > 1:
        parts = [op(parts[n], parts[n + 1]) if n + 1 < len(parts) else parts[n]
                 for n in range(0, len(parts), 2)]
    return parts[0]


def _attn_kernel(qt_ref, qit_ref, wit_ref, k_ref, ki_ref, vt_ref, tri_ref, o_ref,
                 sc_ref, qi_s, wb_s, qp_s, m_s, l_s, acc_s, bias_s, tie_s, ot_s,
                 s0_s, s1_s, mt0_s, mt1_s, p0_s, p1_s, al0_s, al1_s):
    f32 = jnp.float32
    i = pl.program_id(1)
    n_kt = (i * Q_TILE + Q_TILE + K_TILE - 1) // K_TILE
    qpos = i * Q_TILE + lax.broadcasted_iota(jnp.int32, (1, Q_TILE), 1)
    neg_inf = f32(-jnp.inf)
    pos_inf = f32(jnp.inf)
    lane_vec = lambda v: jnp.full((1, Q_TILE), v, f32)

    qi_s[...] = jnp.zeros(qi_s.shape, qi_s.dtype)
    for hd in range(IDX_HEADS):
        qi_s[0:IDX_DIM, hd * Q_TILE:(hd + 1) * Q_TILE] = qit_ref[0, 0, hd * IDX_DIM:(hd + 1) * IDX_DIM, :]
        wb_s[:, hd * Q_TILE:(hd + 1) * Q_TILE] = wit_ref[0, 0, hd:hd + 1, :]

    n_score = (n_kt + SCORE_UNROLL - 1) // SCORE_UNROLL
    n_sweep = (n_kt + SWEEP_UNROLL - 1) // SWEEP_UNROLL
    last_row0 = k_ref.shape[1] - K_TILE

    def score_tile(j, carry):
        rmax, rmin = carry
        row0 = pl.multiple_of(j * K_TILE, K_TILE)
        read0 = pl.multiple_of(jnp.minimum(row0, last_row0), K_TILE)
        lg = _dot(ki_ref[0, pl.ds(read0, K_TILE), :], qi_s[...])
        r = jnp.maximum(lg, 0.0) * wb_s[...]
        s = r[:, 0:Q_TILE]
        for hd in range(1, IDX_HEADS):
            s = s + r[:, hd * Q_TILE:(hd + 1) * Q_TILE]
        kpos = row0 + lax.broadcasted_iota(jnp.int32, (K_TILE, 1), 0)
        causal = kpos <= qpos
        masked = jnp.where(causal, s, neg_inf)
        sc_ref[j] = masked
        rmax = jnp.maximum(rmax, _fold_sublanes(masked, jnp.maximum))
        rmin = jnp.minimum(rmin, _fold_sublanes(jnp.where(causal, s, pos_inf), jnp.minimum))
        return rmax, rmin

    def score_group(jj, carry):
        for u in range(SCORE_UNROLL):
            carry = score_tile(jj * SCORE_UNROLL + u, carry)
        return carry

    rmax, rmin = lax.fori_loop(
        0, n_score, score_group,
        (jnp.full((SUBLANES, Q_TILE), neg_inf, f32), jnp.full((SUBLANES, Q_TILE), pos_inf, f32)))
    rmax = jnp.max(rmax, axis=0, keepdims=True)
    rmin = jnp.min(rmin, axis=0, keepdims=True)

    def fill_tile(j, carry):
        sc_ref[j] = jnp.full((K_TILE, Q_TILE), neg_inf, f32)
        return carry

    lax.fori_loop(n_score * SCORE_UNROLL, n_sweep * SWEEP_UNROLL, fill_tile, 0)

    def count_where(pred_fn):
        def body(jj, acc):
            for u in range(SWEEP_UNROLL):
                t = sc_ref[jj * SWEEP_UNROLL + u]
                acc = acc + _fold_sublanes(jnp.where(pred_fn(t), 1.0, 0.0), jnp.add)
            return acc
        acc = lax.fori_loop(0, n_sweep, body, jnp.zeros((SUBLANES, Q_TILE), f32))
        return jnp.sum(acc, axis=0, keepdims=True)

    def max_below(bound):
        def body(jj, acc):
            for u in range(SWEEP_UNROLL):
                t = sc_ref[jj * SWEEP_UNROLL + u]
                acc = jnp.maximum(
                    acc, _fold_sublanes(jnp.where(t < bound, t, neg_inf), jnp.maximum))
            return acc
        acc = lax.fori_loop(0, n_sweep, body, jnp.full((SUBLANES, Q_TILE), neg_inf, f32))
        return jnp.max(acc, axis=0, keepdims=True)

    k_sel = f32(TOPK_MAX)
    active = qpos >= TOPK_MAX

    def bisect(_, carry):
        lo, hi, c_hi, strict = carry
        mid = lo + 0.5 * (hi - lo)
        c = count_where(lambda t: t >= mid)
        ge = c >= k_sel
        return (jnp.where(ge, mid, lo), jnp.where(ge, hi, mid),
                jnp.where(ge, c_hi, c), jnp.where(ge, strict, 1.0))

    lo, hi, c_hi, strict = lax.fori_loop(
        0, N_BISECT, bisect, (rmin, rmax, lane_vec(0.0), lane_vec(0.0)))

    def walk_cond(carry):
        return jnp.min(carry[0]) < 0.5

    def walk_body(carry):
        done, bound, cnt, thr, need, n_eq = carry
        x = max_below(bound)
        e = count_where(lambda t: t == x)
        finished = jnp.logical_or(cnt + e >= k_sel, x == neg_inf)
        newly = jnp.logical_and(done < 0.5, finished)
        thr = jnp.where(newly, x, thr)
        need = jnp.where(newly, k_sel - cnt, need)
        n_eq = jnp.where(newly, e, n_eq)
        done = jnp.where(newly, 1.0, done)
        still = done < 0.5
        return (done, jnp.where(still, x, bound), jnp.where(still, cnt + e, cnt), thr, need, n_eq)

    is_strict = strict > 0.5
    _, _, _, thr, need, n_eq = lax.while_loop(
        walk_cond, walk_body,
        (jnp.where(active, 0.0, 1.0), jnp.where(is_strict, hi, pos_inf),
         jnp.where(is_strict, c_hi, 0.0), lane_vec(0.0), lane_vec(0.0), lane_vec(0.0)))
    thr = jnp.where(active, thr, rmin)
    need = jnp.where(active, need, f32(2 ** 30))
    has_ties = jnp.max(jnp.where(jnp.logical_and(active, n_eq > need), 1.0, 0.0)) > 0.5

    qp_s[...] = jnp.zeros(qp_s.shape, qp_s.dtype)
    for p in range(N_PAIRS):
        qp_s[p, 0:HEAD_DIM, 0:Q_TILE] = qt_ref[0, 0, (2 * p) * HEAD_DIM:(2 * p + 1) * HEAD_DIM, :]
        qp_s[p, HEAD_DIM:LANES, Q_TILE:2 * Q_TILE] = qt_ref[
            0, 0, (2 * p + 1) * HEAD_DIM:(2 * p + 2) * HEAD_DIM, :]
    m_s[...] = jnp.full(m_s.shape, neg_inf, f32)
    l_s[...] = jnp.zeros(l_s.shape, f32)
    acc_s[...] = jnp.zeros(acc_s.shape, f32)
    tie_s[...] = jnp.zeros(tie_s.shape, f32)

    last_tile = n_kt - 1
    ones_rows = jnp.ones((DENOM_ROWS, K_TILE), MXU_DTYPE)

    def attend(tie_aware):
        def selection_bias(j):
            thr_j = jnp.where(j <= last_tile, thr, pos_inf)
            t = sc_ref[jnp.minimum(j, last_tile)]

            if not tie_aware:
                bias_s[...] = jnp.where(t >= thr_j, 0.0, neg_inf)
            else:
                eq = t == thr_j
                eqf = jnp.where(eq, 1.0, 0.0)
                rank = tie_s[...] + _dot(tri_ref[...], eqf.astype(MXU_DTYPE))
                sel = jnp.logical_or(t > thr_j, jnp.logical_and(eq, rank < need))
                bias_s[...] = jnp.where(sel, 0.0, neg_inf)
                tie_s[...] = tie_s[...] + jnp.sum(eqf, axis=0, keepdims=True)

        def logits(p, j, s_buf, mt_buf):
            g = (2 * p) // HEADS_PER_KV
            row0 = pl.multiple_of(jnp.minimum(j, last_tile) * K_TILE, K_TILE)
            s = _dot(k_ref[0, pl.ds(row0, K_TILE), g * LANES:(g + 1) * LANES], qp_s[p])
            for half in range(2):
                sh = s[:, half * Q_TILE:(half + 1) * Q_TILE] + bias_s[...]
                s_buf[p, :, half * Q_TILE:(half + 1) * Q_TILE] = sh
                mt_buf[p, :, half * Q_TILE:(half + 1) * Q_TILE] = jnp.max(
                    _fold_sublanes(sh, jnp.maximum), axis=0, keepdims=True)

        def numerators(p, s_buf, mt_buf, p_buf, al_buf):
            m_old = m_s[p]
            m_new = jnp.maximum(m_old, mt_buf[p])
            m_safe = jnp.where(m_new == neg_inf, 0.0, m_new)
            al_buf[p] = jnp.exp2(m_old - m_safe)
            m_s[p] = m_new
            for r0 in range(0, K_TILE, EXP_ROWS):
                d = (s_buf[p, r0:r0 + EXP_ROWS, :] - m_safe).astype(MXU_DTYPE)
                p_buf[p, r0:r0 + EXP_ROWS, :] = jnp.exp2(d)

        def weighted_values(p, j, p_buf, al_buf):
            g = (2 * p) // HEADS_PER_KV
            lhs = jnp.concatenate(
                [vt_ref[0, jnp.clip(j, 0, last_tile), g * HEAD_DIM:(g + 1) * HEAD_DIM, :], ones_rows],
                axis=0)
            pv = _dot(lhs, p_buf[p])
            acc_s[p] = al_buf[p] * acc_s[p] + pv[0:HEAD_DIM]
            l_s[p] = al_buf[p] * l_s[p] + pv[HEAD_DIM:HEAD_DIM + 1]

        def stage(j, s_next, mt_next, s_cur, mt_cur, p_cur, al_cur, p_prev, al_prev):
            selection_bias(j + 1)
            for p in range(N_PAIRS):
                logits(p, j + 1, s_next, mt_next)
                numerators(p, s_cur, mt_cur, p_cur, al_cur)
                weighted_values(p, j - 1, p_prev, al_prev)

        p1_s[...] = jnp.zeros(p1_s.shape, p1_s.dtype)
        al1_s[...] = jnp.ones(al1_s.shape, f32)
        selection_bias(0)
        for p in range(N_PAIRS):
            logits(p, 0, s0_s, mt0_s)

        def attend_two_tiles(jj, carry):
            j = 2 * jj
            stage(j, s1_s, mt1_s, s0_s, mt0_s, p0_s, al0_s, p1_s, al1_s)
            stage(j + 1, s0_s, mt0_s, s1_s, mt1_s, p1_s, al1_s, p0_s, al0_s)
            return carry

        n_two = (n_kt + 1) // 2
        lax.fori_loop(0, n_two, attend_two_tiles, 0)
        for p in range(N_PAIRS):
            weighted_values(p, 2 * n_two - 1, p1_s, al1_s)

    @pl.when(jnp.logical_not(has_ties))
    def _():
        attend(False)

    @pl.when(has_ties)
    def _():
        attend(True)

    for p in range(N_PAIRS):
        o = acc_s[p] / l_s[p]
        ot_s[(2 * p) * HEAD_DIM:(2 * p + 1) * HEAD_DIM, :] = o[:, 0:Q_TILE]
        ot_s[(2 * p + 1) * HEAD_DIM:(2 * p + 2) * HEAD_DIM, :] = o[:, Q_TILE:2 * Q_TILE]
    o_ref[0] = ot_s[...].T.astype(o_ref.dtype)


def _attn(qt, qit, wit, k, ki, vt, tri):
    B, nq, _, _ = qt.shape
    Tp = nq * Q_TILE
    nkt = Tp // K_TILE
    f32 = jnp.float32
    per_q = lambda rows: pl.BlockSpec((1, 1, rows, Q_TILE), lambda b, i: (b, i, 0, 0))
    once = pl.Buffered(1)
    return pl.pallas_call(
        _attn_kernel,
        grid=(B, nq),
        in_specs=[
            per_q(N_HEADS * HEAD_DIM),
            per_q(IDX_HEADS * IDX_DIM),
            per_q(IDX_HEADS),
            pl.BlockSpec((1, Tp, N_KV_HEADS * LANES), lambda b, i: (b, 0, 0), pipeline_mode=once),
            pl.BlockSpec((1, Tp, LANES), lambda b, i: (b, 0, 0), pipeline_mode=once),
            pl.BlockSpec((1, nkt, N_KV_HEADS * HEAD_DIM, K_TILE), lambda b, i: (b, 0, 0, 0),
                         pipeline_mode=once),
            pl.BlockSpec((K_TILE, K_TILE), lambda b, i: (0, 0), pipeline_mode=once),
        ],
        out_specs=pl.BlockSpec((1, Q_TILE, N_HEADS * HEAD_DIM), lambda b, i: (b, i, 0)),
        out_shape=jax.ShapeDtypeStruct((B, Tp, N_HEADS * HEAD_DIM), MXU_DTYPE),
        scratch_shapes=[
            pltpu.VMEM((-(-nkt // SWEEP_UNROLL) * SWEEP_UNROLL, K_TILE, Q_TILE), f32),
            pltpu.VMEM((LANES, IDX_HEADS * Q_TILE), MXU_DTYPE),
            pltpu.VMEM((1, IDX_HEADS * Q_TILE), f32),
            pltpu.VMEM((N_PAIRS, LANES, 2 * Q_TILE), MXU_DTYPE),
            pltpu.VMEM((N_PAIRS, 1, 2 * Q_TILE), f32),
            pltpu.VMEM((N_PAIRS, 1, 2 * Q_TILE), f32),
            pltpu.VMEM((N_PAIRS, HEAD_DIM, 2 * Q_TILE), f32),
            pltpu.VMEM((K_TILE, Q_TILE), f32),
            pltpu.VMEM((1, Q_TILE), f32),
            pltpu.VMEM((N_HEADS * HEAD_DIM, Q_TILE), f32),
            pltpu.VMEM((N_PAIRS, K_TILE, 2 * Q_TILE), f32),
            pltpu.VMEM((N_PAIRS, K_TILE, 2 * Q_TILE), f32),
            pltpu.VMEM((N_PAIRS, 1, 2 * Q_TILE), f32),
            pltpu.VMEM((N_PAIRS, 1, 2 * Q_TILE), f32),
            pltpu.VMEM((N_PAIRS, K_TILE, 2 * Q_TILE), MXU_DTYPE),
            pltpu.VMEM((N_PAIRS, K_TILE, 2 * Q_TILE), MXU_DTYPE),
            pltpu.VMEM((N_PAIRS, 1, 2 * Q_TILE), f32),
            pltpu.VMEM((N_PAIRS, 1, 2 * Q_TILE), f32),
        ],
        compiler_params=pltpu.CompilerParams(
            dimension_semantics=("arbitrary", "arbitrary"), vmem_limit_bytes=VMEM_LIMIT),
        name="attn",
    )(qt, qit, wit, k, ki, vt, tri)


def _mix_kernel(o_ref, u_ref, up_ref, ga_ref, gc_ref, h_ref, wa_ref, wc_ref, wo_ref,
                cw_ref, cb_ref, lg_ref, lb_ref, out_ref, ux_ref, sh_ref, cv_ref):
    f32 = jnp.float32
    i = pl.program_id(1)
    prev = up_ref[0].astype(f32)
    ux_ref[0:CONV_HALO, :] = jnp.where(i > 0, prev, jnp.zeros_like(prev))
    ux_ref[CONV_HALO:CONV_HALO + MIX_TILE, :] = u_ref[0].astype(f32)
    ext = CONV_HALO + MIX_TILE
    for c0 in range(0, D_MODEL, CONV_COLS):
        cols = slice(c0, c0 + CONV_COLS)
        for r in range(1, SUBLANES):
            sh_ref[r - 1, SUBLANES:ext, :] = ux_ref[SUBLANES - r:ext - r, cols]
        acc = jnp.broadcast_to(cb_ref[:, cols], (MIX_TILE, CONV_COLS))
        for j in range(CONV_WIDTH):
            a, r = divmod(CONV_WIDTH - 1 - j, SUBLANES)
            start = CONV_HALO - SUBLANES * a
            src = ux_ref[start:start + MIX_TILE, cols] if r == 0 else sh_ref[
                r - 1, start:start + MIX_TILE, :]
            acc = acc + cw_ref[j:j + 1, cols] * src
        cv_ref[:, cols] = acc
    acc = cv_ref[...]
    mu = jnp.mean(acc, axis=-1, keepdims=True)
    cen = acc - mu
    var = jnp.mean(cen * cen, axis=-1, keepdims=True)
    y = cen * lax.rsqrt(var + LN_EPS) * lg_ref[...] + lb_ref[...]
    y = y * _sigmoid(y)
    y_conv = _dot(y.astype(MXU_DTYPE), wc_ref[...])
    y_attn = _dot(o_ref[0], wa_ref[...])
    merged = ga_ref[0].astype(f32) * y_attn + gc_ref[0].astype(f32) * y_conv
    out_ref[0] = h_ref[0] + _dot(merged.astype(MXU_DTYPE), wo_ref[...])


def _mix(o, u, ga, gc, h, wa, wc, wo, cw, cb, lg, lb):
    B, Tp, D = h.shape
    row = pl.BlockSpec((1, MIX_TILE, D), lambda b, i: (b, i, 0))
    halo_blocks = MIX_TILE // CONV_HALO
    halo = pl.BlockSpec((1, CONV_HALO, D),
                        lambda b, i: (b, jnp.maximum(i * halo_blocks - 1, 0), 0))
    const = lambda shape: pl.BlockSpec(shape, lambda b, i: (0,) * len(shape),
                                       pipeline_mode=pl.Buffered(1))
    return pl.pallas_call(
        _mix_kernel,
        grid=(B, Tp // MIX_TILE),
        in_specs=[row, row, halo, row, row, row,
                  const((D, D)), const((D, D)), const((D, D)),
                  const((CONV_WIDTH, D)), const((1, D)), const((1, D)), const((1, D))],
        out_specs=row,
        out_shape=jax.ShapeDtypeStruct((B, Tp, D), jnp.float32),
        scratch_shapes=[
            pltpu.VMEM((CONV_HALO + MIX_TILE, D), jnp.float32),
            pltpu.VMEM((SUBLANES - 1, CONV_HALO + MIX_TILE, CONV_COLS), jnp.float32),
            pltpu.VMEM((MIX_TILE, D), jnp.float32),
        ],
        compiler_params=pltpu.CompilerParams(
            dimension_semantics=("arbitrary", "arbitrary"), vmem_limit_bytes=VMEM_LIMIT),
        name="mix",
    )(o, u, u, ga, gc, h, wa, wc, wo, cw, cb, lg, lb)


def _ffn_kernel(h_ref, hp_ref, g_ref, wu_ref, cw_ref, cb_ref, wd_ref, out_ref, xn_ref, act_ref):
    i = pl.program_id(1)

    def norm(x):
        ms = jnp.mean(x * x, axis=-1, keepdims=True)
        return x * lax.rsqrt(ms + RMS_EPS) * g_ref[...]

    prev = norm(hp_ref[0])
    xn_ref[0:FFN_HALO, :] = jnp.where(i > 0, prev, jnp.zeros_like(prev)).astype(xn_ref.dtype)
    x = h_ref[0]
    xn_ref[FFN_HALO:FFN_HALO + FFN_TILE, :] = norm(x).astype(xn_ref.dtype)
    xn = xn_ref[...]

    def conv(c0):
        hcol = _dot(xn, wu_ref[:, c0:c0 + FFN_CHUNK])
        out = jnp.broadcast_to(cb_ref[:, c0:c0 + FFN_CHUNK], (FFN_TILE, FFN_CHUNK))
        for j in range(FFN_CONV_WIDTH):
            back = FFN_CONV_WIDTH - 1 - j
            shifted = hcol if back == 0 else pltpu.roll(hcol, back, 0)
            out = out + cw_ref[j:j + 1, c0:c0 + FFN_CHUNK] * shifted[FFN_HALO:FFN_HALO + FFN_TILE]
        return out

    for c0 in range(0, FFN_HIDDEN, FFN_CHUNK):
        gate = conv(c0)
        up = conv(FFN_HIDDEN + c0)
        act_ref[:, c0:c0 + FFN_CHUNK] = (gate * _sigmoid(gate) * up).astype(act_ref.dtype)
    out_ref[0] = x + _dot(act_ref[...], wd_ref[...])


def _ffn(h, g, wu, cw, cb, wd):
    B, Tp, D = h.shape
    row = pl.BlockSpec((1, FFN_TILE, D), lambda b, i: (b, i, 0))
    halo_blocks = FFN_TILE // FFN_HALO
    halo = pl.BlockSpec((1, FFN_HALO, D),
                        lambda b, i: (b, jnp.maximum(i * halo_blocks - 1, 0), 0))
    const = lambda shape: pl.BlockSpec(shape, lambda b, i: (0,) * len(shape),
                                       pipeline_mode=pl.Buffered(1))
    return pl.pallas_call(
        _ffn_kernel,
        grid=(B, Tp // FFN_TILE),
        in_specs=[row, halo, const((1, D)), const((D, 2 * FFN_HIDDEN)),
                  const((FFN_CONV_WIDTH, 2 * FFN_HIDDEN)), const((1, 2 * FFN_HIDDEN)),
                  const((FFN_HIDDEN, D))],
        out_specs=row,
        out_shape=jax.ShapeDtypeStruct((B, Tp, D), jnp.float32),
        scratch_shapes=[pltpu.VMEM((FFN_HALO + FFN_TILE, D), MXU_DTYPE),
                        pltpu.VMEM((FFN_TILE, FFN_HIDDEN), MXU_DTYPE)],
        compiler_params=pltpu.CompilerParams(
            dimension_semantics=("arbitrary", "arbitrary"), vmem_limit_bytes=VMEM_LIMIT),
        name="ffn",
    )(h, h, g, wu, cw, cb, wd)


def _final_norm_kernel(h_ref, g_ref, out_ref):
    x = h_ref[0]
    ms = jnp.mean(x * x, axis=-1, keepdims=True)
    out_ref[0] = x * lax.rsqrt(ms + RMS_EPS) * g_ref[...]


def _final_norm(h, g):
    B, Tp, D = h.shape
    row = pl.BlockSpec((1, ROW_TILE, D), lambda b, i: (b, i, 0))
    return pl.pallas_call(
        _final_norm_kernel,
        grid=(B, Tp // ROW_TILE),
        in_specs=[row, pl.BlockSpec((1, D), lambda b, i: (0, 0))],
        out_specs=row,
        out_shape=jax.ShapeDtypeStruct((B, Tp, D), jnp.float32),
        compiler_params=pltpu.CompilerParams(
            dimension_semantics=("arbitrary", "arbitrary"), vmem_limit_bytes=VMEM_LIMIT),
        name="final_norm",
    )(h, g)


def _rope_tables(Tp):
    half = ROT_DIM // 2
    pos = jnp.arange(Tp, dtype=jnp.float32)
    inv_freq = jnp.power(jnp.float32(ROPE_THETA),
                         -jnp.arange(0, ROT_DIM, 2, dtype=jnp.float32) / ROT_DIM)
    ang = pos[:, None] * inv_freq[None, :]
    cos, sin = jnp.cos(ang), jnp.sin(ang)
    zeros = lambda n: jnp.zeros((Tp, n), jnp.float32)
    cos_h = jnp.concatenate([cos, cos, jnp.ones((Tp, HEAD_DIM - ROT_DIM), jnp.float32)], axis=1)
    sina_h = jnp.concatenate([-sin, zeros(HEAD_DIM - half)], axis=1)
    sinb_h = jnp.concatenate([zeros(half), sin, zeros(HEAD_DIM - ROT_DIM)], axis=1)
    two = lambda t: jnp.concatenate([t, t], axis=1)
    return two(cos_h), two(sina_h), two(sinb_h)


def _pack_w_in(w):
    sizes = (N_HEADS * HEAD_DIM, N_KV_HEADS * HEAD_DIM, N_KV_HEADS * HEAD_DIM,
             IDX_HEADS * IDX_DIM, IDX_DIM, IDX_HEADS, 2 * D_MODEL, D_MODEL, D_MODEL)
    offs = [0]
    for s in sizes:
        offs.append(offs[-1] + s)
    wq, wk, wv, wqi, wki, wwi, wglu, wga, wgc = (w[:, offs[n]:offs[n + 1]] for n in range(9))

    def dup_heads(m):
        n = m.shape[1] // HEAD_DIM
        m = m.reshape(m.shape[0], n, 1, HEAD_DIM)
        return jnp.broadcast_to(m, (m.shape[0], n, 2, HEAD_DIM)).reshape(m.shape[0], n * LANES)

    pad_lanes = lambda m: jnp.pad(m, ((0, 0), (0, LANES - m.shape[1])))
    q_scale = (HEAD_DIM ** -0.5) * math.log2(math.e)
    cols = [wq * q_scale, wqi * (IDX_DIM ** -0.5), dup_heads(wk), pad_lanes(wki), wv,
            pad_lanes(wwi), wglu, wga, wgc]
    return jnp.concatenate(cols, axis=1).astype(MXU_DTYPE)


def kernel(x, meta_tokens, attn_norm_g, w_in, w_attn_out, conv_dw_w, conv_dw_b, conv_ln_g,
           conv_ln_b, w_conv_out, w_o, ffn_norm_g, w_up, ffn_dw_w, ffn_dw_b, w_down,
           final_norm_g):
    B, S, D = x.shape
    T = N_META + S
    Tp = -(-T // ROW_TILE) * ROW_TILE
    bf = MXU_DTYPE
    meta = jnp.broadcast_to(meta_tokens[None].astype(x.dtype), (B, N_META, D))
    h = jnp.concatenate([meta, x, jnp.zeros((B, Tp - T, D), x.dtype)], axis=1)
    cos, sina, sinb = _rope_tables(Tp)
    tri = (lax.broadcasted_iota(jnp.int32, (K_TILE, K_TILE), 1)
           < lax.broadcasted_iota(jnp.int32, (K_TILE, K_TILE), 0)).astype(bf)
    row = lambda v: v.reshape(1, -1)
    for l in range(w_in.shape[0]):
        qt, qit, k, ki, vt, wit, u, ga, gc = _proj(h, row(attn_norm_g[l]), _pack_w_in(w_in[l]),
                                                   cos, sina, sinb)
        o = _attn(qt, qit, wit, k, ki, vt, tri)
        h = _mix(o, u, ga, gc, h, w_attn_out[l].astype(bf), w_conv_out[l].astype(bf),
                 w_o[l].astype(bf), conv_dw_w[l], row(conv_dw_b[l]), row(conv_ln_g[l]),
                 row(conv_ln_b[l]))
        h = _ffn(h, row(ffn_norm_g[l]), w_up[l].astype(bf), ffn_dw_w[l], row(ffn_dw_b[l]),
                 w_down[l].astype(bf))
    return _final_norm(h, row(final_norm_g))[:, N_META:T]
```

```python
import functools
import math

import jax
import jax.numpy as jnp
from jax import lax
from jax.experimental import pallas as pl
from jax.experimental.pallas import tpu as pltpu

D_MODEL = 1024
N_META = 16
N_HEADS = 16
HEAD_DIM = 64
N_KV_HEADS = 4
HEADS_PER_KV = N_HEADS // N_KV_HEADS
N_PAIRS = N_HEADS // 2
IDX_HEADS = 8
IDX_DIM = 64
TOPK_MAX = 256
ROPE_THETA = 500000.0
ROT_DIM = HEAD_DIM // 4
CONV_WIDTH = 31
FFN_HIDDEN = 2816
FFN_CONV_WIDTH = 3
RMS_EPS = 1e-6
LN_EPS = 1e-5

LANES = 128
SUBLANES = 8
Q_TILE = 128
K_TILE = 256
ROW_TILE = 768
MIX_TILE = 384
FFN_TILE = 384
CONV_HALO = 32
CONV_COLS = 256
FFN_HALO = 16
FFN_CHUNK = 256
N_BISECT = 14
SCORE_UNROLL = 4
SWEEP_UNROLL = 4
EXP_ROWS = 32
DENOM_ROWS = 16
VMEM_LIMIT = 56 * 1024 * 1024
MXU_DTYPE = jnp.bfloat16

_C_Q = 0
_C_QI = _C_Q + N_HEADS * HEAD_DIM
_C_K = _C_QI + IDX_HEADS * IDX_DIM
_C_KI = _C_K + N_KV_HEADS * LANES
_C_V = _C_KI + LANES
_C_WI = _C_V + N_KV_HEADS * HEAD_DIM
_C_GLU = _C_WI + LANES
_C_GATE = _C_GLU + 2 * D_MODEL
_C_END = _C_GATE + 2 * D_MODEL


def _sigmoid(x):
    return 1.0 / (1.0 + jnp.exp(-x))


def _dot(a, b):
    return jnp.dot(a, b, preferred_element_type=jnp.float32)


def _proj_kernel(h_ref, g_ref, w_ref, cos_ref, sina_ref, sinb_ref,
                 qt_ref, qit_ref, k_ref, ki_ref, vt_ref, wit_ref, u_ref, ga_ref, gc_ref):
    x = h_ref[0]
    ms = jnp.mean(x * x, axis=-1, keepdims=True)
    xn = (x * lax.rsqrt(ms + RMS_EPS) * g_ref[...]).astype(MXU_DTYPE)
    cos = cos_ref[...]
    sina = sina_ref[...]
    sinb = sinb_ref[...]
    tiles = ROW_TILE // Q_TILE

    def rope(r):
        return (r * cos + pltpu.roll(r, LANES - ROT_DIM // 2, 1) * sina
                + pltpu.roll(r, ROT_DIM // 2, 1) * sinb)

    def proj(c0, width):
        return _dot(xn, w_ref[:, c0:c0 + width])

    def store_transposed(ref, row0, val):
        vt = val.T.astype(ref.dtype)
        for t in range(tiles):
            ref[0, t, row0:row0 + LANES, :] = vt[:, t * Q_TILE:(t + 1) * Q_TILE]

    for c0 in range(_C_Q, _C_QI, 2 * LANES):
        r = proj(c0, 2 * LANES)
        for s in range(2):
            store_transposed(qt_ref, c0 - _C_Q + s * LANES, rope(r[:, s * LANES:(s + 1) * LANES]))
    for c0 in range(_C_QI, _C_K, 2 * LANES):
        r = proj(c0, 2 * LANES)
        for s in range(2):
            store_transposed(qit_ref, c0 - _C_QI + s * LANES, rope(r[:, s * LANES:(s + 1) * LANES]))
    for c0 in range(_C_K, _C_KI, 2 * LANES):
        r = proj(c0, 2 * LANES)
        for s in range(2):
            col = c0 - _C_K + s * LANES
            k_ref[0, :, col:col + LANES] = rope(r[:, s * LANES:(s + 1) * LANES]).astype(k_ref.dtype)
    ki_ref[0] = rope(proj(_C_KI, LANES)).astype(ki_ref.dtype)
    v = proj(_C_V, 2 * LANES)
    for s in range(2):
        vt = v[:, s * LANES:(s + 1) * LANES].T.astype(vt_ref.dtype)
        for t in range(ROW_TILE // K_TILE):
            vt_ref[0, t, s * LANES:(s + 1) * LANES, :] = vt[:, t * K_TILE:(t + 1) * K_TILE]
    wit = (proj(_C_WI, LANES) * (IDX_HEADS ** -0.5)).T
    for t in range(tiles):
        wit_ref[0, t] = wit[0:IDX_HEADS, t * Q_TILE:(t + 1) * Q_TILE]
    for c0 in range(0, D_MODEL, 2 * LANES):
        a = proj(_C_GLU + c0, 2 * LANES)
        b = proj(_C_GLU + D_MODEL + c0, 2 * LANES)
        u_ref[0, :, c0:c0 + 2 * LANES] = (a * _sigmoid(b)).astype(u_ref.dtype)
    for c0 in range(0, D_MODEL, 2 * LANES):
        ga_ref[0, :, c0:c0 + 2 * LANES] = _sigmoid(proj(_C_GATE + c0, 2 * LANES)).astype(ga_ref.dtype)
        gc_ref[0, :, c0:c0 + 2 * LANES] = _sigmoid(
            proj(_C_GATE + D_MODEL + c0, 2 * LANES)).astype(gc_ref.dtype)


def _proj(h, g, w, cos, sina, sinb):
    B, Tp, D = h.shape
    nt = Tp // ROW_TILE
    nq = Tp // Q_TILE
    nk = Tp // K_TILE
    qpt = ROW_TILE // Q_TILE
    kpt = ROW_TILE // K_TILE
    bf = MXU_DTYPE
    row = lambda width: pl.BlockSpec((1, ROW_TILE, width), lambda b, i: (b, i, 0))
    tab = pl.BlockSpec((ROW_TILE, LANES), lambda b, i: (i, 0))
    per_qtile = lambda rows: pl.BlockSpec((1, qpt, rows, Q_TILE), lambda b, i: (b, i, 0, 0))
    out_shape = (
        jax.ShapeDtypeStruct((B, nq, N_HEADS * HEAD_DIM, Q_TILE), bf),
        jax.ShapeDtypeStruct((B, nq, IDX_HEADS * IDX_DIM, Q_TILE), bf),
        jax.ShapeDtypeStruct((B, Tp, N_KV_HEADS * LANES), bf),
        jax.ShapeDtypeStruct((B, Tp, LANES), bf),
        jax.ShapeDtypeStruct((B, nk, N_KV_HEADS * HEAD_DIM, K_TILE), bf),
        jax.ShapeDtypeStruct((B, nq, IDX_HEADS, Q_TILE), jnp.float32),
        jax.ShapeDtypeStruct((B, Tp, D), bf),
        jax.ShapeDtypeStruct((B, Tp, D), bf),
        jax.ShapeDtypeStruct((B, Tp, D), bf),
    )
    out_specs = (
        per_qtile(N_HEADS * HEAD_DIM),
        per_qtile(IDX_HEADS * IDX_DIM),
        row(N_KV_HEADS * LANES),
        row(LANES),
        pl.BlockSpec((1, kpt, N_KV_HEADS * HEAD_DIM, K_TILE), lambda b, i: (b, i, 0, 0)),
        per_qtile(IDX_HEADS),
        row(D), row(D), row(D),
    )
    return pl.pallas_call(
        _proj_kernel,
        grid=(B, nt),
        in_specs=[
            row(D),
            pl.BlockSpec((1, D), lambda b, i: (0, 0)),
            pl.BlockSpec((D, _C_END), lambda b, i: (0, 0), pipeline_mode=pl.Buffered(1)),
            tab, tab, tab,
        ],
        out_specs=out_specs,
        out_shape=out_shape,
        compiler_params=pltpu.CompilerParams(
            dimension_semantics=("arbitrary", "arbitrary"), vmem_limit_bytes=VMEM_LIMIT),
        name="proj",
    )(h, g, w, cos, sina, sinb)


def _fold_sublanes(x, op):
    parts = [x[r:r + SUBLANES] for r in range(0, x.shape[0], SUBLANES)]
    while len(parts) > 1:
        parts = [op(parts[n], parts[n + 1]) if n + 1 < len(parts) else parts[n]
                 for n in range(0, len(parts), 2)]
    return parts[0]


def _attn_kernel(qt_ref, qit_ref, wit_ref, k_ref, ki_ref, vt_ref, tri_ref, o_ref,
                 sc_ref, qi_s, wb_s, qp_s, m_s, l_s, acc_s, bias_s, tie_s, ot_s,
                 s0_s, s1_s, mt0_s, mt1_s, p0_s, p1_s, al0_s, al1_s):
    f32 = jnp.float32
    i = pl.program_id(1)
    n_kt = (i * Q_TILE + Q_TILE + K_TILE - 1) // K_TILE
    qpos = i * Q_TILE + lax.broadcasted_iota(jnp.int32, (1, Q_TILE), 1)
    neg_inf = f32(-jnp.inf)
    pos_inf = f32(jnp.inf)
    lane_vec = lambda v: jnp.full((1, Q_TILE), v, f32)

    qi_s[...] = jnp.zeros(qi_s.shape, qi_s.dtype)
    for hd in range(IDX_HEADS):
        qi_s[0:IDX_DIM, hd * Q_TILE:(hd + 1) * Q_TILE] = qit_ref[0, 0, hd * IDX_DIM:(hd + 1) * IDX_DIM, :]
        wb_s[:, hd * Q_TILE:(hd + 1) * Q_TILE] = wit_ref[0, 0, hd:hd + 1, :]

    n_score = (n_kt + SCORE_UNROLL - 1) // SCORE_UNROLL
    n_sweep = (n_kt + SWEEP_UNROLL - 1) // SWEEP_UNROLL
    last_row0 = k_ref.shape[1] - K_TILE

    def score_tile(j, carry):
        rmax, rmin = carry
        row0 = pl.multiple_of(j * K_TILE, K_TILE)
        read0 = pl.multiple_of(jnp.minimum(row0, last_row0), K_TILE)
        lg = _dot(ki_ref[0, pl.ds(read0, K_TILE), :], qi_s[...])
        r = jnp.maximum(lg, 0.0) * wb_s[...]
        s = r[:, 0:Q_TILE]
        for hd in range(1, IDX_HEADS):
            s = s + r[:, hd * Q_TILE:(hd + 1) * Q_TILE]
        kpos = row0 + lax.broadcasted_iota(jnp.int32, (K_TILE, 1), 0)
        causal = kpos <= qpos
        masked = jnp.where(causal, s, neg_inf)
        sc_ref[j] = masked
        rmax = jnp.maximum(rmax, _fold_sublanes(masked, jnp.maximum))
        rmin = jnp.minimum(rmin, _fold_sublanes(jnp.where(causal, s, pos_inf), jnp.minimum))
        return rmax, rmin

    def score_group(jj, carry):
        for u in range(SCORE_UNROLL):
            carry = score_tile(jj * SCORE_UNROLL + u, carry)
        return carry

    rmax, rmin = lax.fori_loop(
        0, n_score, score_group,
        (jnp.full((SUBLANES, Q_TILE), neg_inf, f32), jnp.full((SUBLANES, Q_TILE), pos_inf, f32)))
    rmax = jnp.max(rmax, axis=0, keepdims=True)
    rmin = jnp.min(rmin, axis=0, keepdims=True)

    def fill_tile(j, carry):
        sc_ref[j] = jnp.full((K_TILE, Q_TILE), neg_inf, f32)
        return carry

    lax.fori_loop(n_score * SCORE_UNROLL, n_sweep * SWEEP_UNROLL, fill_tile, 0)

    def count_where(pred_fn):
        def body(jj, acc):
            for u in range(SWEEP_UNROLL):
                t = sc_ref[jj * SWEEP_UNROLL + u]
                acc = acc + _fold_sublanes(jnp.where(pred_fn(t), 1.0, 0.0), jnp.add)
            return acc
        acc = lax.fori_loop(0, n_sweep, body, jnp.zeros((SUBLANES, Q_TILE), f32))
        return jnp.sum(acc, axis=0, keepdims=True)

    def max_below(bound):
        def body(jj, acc):
            for u in range(SWEEP_UNROLL):
                t = sc_ref[jj * SWEEP_UNROLL + u]
                acc = jnp.maximum(
                    acc, _fold_sublanes(jnp.where(t < bound, t, neg_inf), jnp.maximum))
            return acc
        acc = lax.fori_loop(0, n_sweep, body, jnp.full((SUBLANES, Q_TILE), neg_inf, f32))
        return jnp.max(acc, axis=0, keepdims=True)

    k_sel = f32(TOPK_MAX)
    active = qpos >= TOPK_MAX

    def bisect(_, carry):
        lo, hi, c_hi, strict = carry
        mid = lo + 0.5 * (hi - lo)
        c = count_where(lambda t: t >= mid)
        ge = c >= k_sel
        return (jnp.where(ge, mid, lo), jnp.where(ge, hi, mid),
                jnp.where(ge, c_hi, c), jnp.where(ge, strict, 1.0))

    lo, hi, c_hi, strict = lax.fori_loop(
        0, N_BISECT, bisect, (rmin, rmax, lane_vec(0.0), lane_vec(0.0)))

    def walk_cond(carry):
        return jnp.min(carry[0]) < 0.5

    def walk_body(carry):
        done, bound, cnt, thr, need, n_eq = carry
        x = max_below(bound)
        e = count_where(lambda t: t == x)
        finished = jnp.logical_or(cnt + e >= k_sel, x == neg_inf)
        newly = jnp.logical_and(done < 0.5, finished)
        thr = jnp.where(newly, x, thr)
        need = jnp.where(newly, k_sel - cnt, need)
        n_eq = jnp.where(newly, e, n_eq)
        done = jnp.where(newly, 1.0, done)
        still = done < 0.5
        return (done, jnp.where(still, x, bound), jnp.where(still, cnt + e, cnt), thr, need, n_eq)

    is_strict = strict > 0.5
    _, _, _, thr, need, n_eq = lax.while_loop(
        walk_cond, walk_body,
        (jnp.where(active, 0.0, 1.0), jnp.where(is_strict, hi, pos_inf),
         jnp.where(is_strict, c_hi, 0.0), lane_vec(0.0), lane_vec(0.0), lane_vec(0.0)))
    thr = jnp.where(active, thr, rmin)
    need = jnp.where(active, need, f32(2 ** 30))
    has_ties = jnp.max(jnp.where(jnp.logical_and(active, n_eq > need), 1.0, 0.0)) > 0.5

    qp_s[...] = jnp.zeros(qp_s.shape, qp_s.dtype)
    for p in range(N_PAIRS):
        qp_s[p, 0:HEAD_DIM, 0:Q_TILE] = qt_ref[0, 0, (2 * p) * HEAD_DIM:(2 * p + 1) * HEAD_DIM, :]
        qp_s[p, HEAD_DIM:LANES, Q_TILE:2 * Q_TILE] = qt_ref[
            0, 0, (2 * p + 1) * HEAD_DIM:(2 * p + 2) * HEAD_DIM, :]
    m_s[...] = jnp.full(m_s.shape, neg_inf, f32)
    l_s[...] = jnp.zeros(l_s.shape, f32)
    acc_s[...] = jnp.zeros(acc_s.shape, f32)
    tie_s[...] = jnp.zeros(tie_s.shape, f32)

    last_tile = n_kt - 1
    ones_rows = jnp.ones((DENOM_ROWS, K_TILE), MXU_DTYPE)

    def attend(tie_aware):
        def selection_bias(j):
            thr_j = jnp.where(j <= last_tile, thr, pos_inf)
            t = sc_ref[jnp.minimum(j, last_tile)]

            if not tie_aware:
                bias_s[...] = jnp.where(t >= thr_j, 0.0, neg_inf)
            else:
                eq = t == thr_j
                eqf = jnp.where(eq, 1.0, 0.0)
                rank = tie_s[...] + _dot(tri_ref[...], eqf.astype(MXU_DTYPE))
                sel = jnp.logical_or(t > thr_j, jnp.logical_and(eq, rank < need))
                bias_s[...] = jnp.where(sel, 0.0, neg_inf)
                tie_s[...] = tie_s[...] + jnp.sum(eqf, axis=0, keepdims=True)

        def logits(p, j, s_buf, mt_buf):
            g = (2 * p) // HEADS_PER_KV
            row0 = pl.multiple_of(jnp.minimum(j, last_tile) * K_TILE, K_TILE)
            s = _dot(k_ref[0, pl.ds(row0, K_TILE), g * LANES:(g + 1) * LANES], qp_s[p])
            for half in range(2):
                cols = slice(half * Q_TILE, (half + 1) * Q_TILE)
                run = None
                for r0 in range(0, K_TILE, EXP_ROWS):
                    sh = s[r0:r0 + EXP_ROWS, cols] + bias_s[r0:r0 + EXP_ROWS, :]
                    s_buf[p, r0:r0 + EXP_ROWS, cols] = sh
                    run = sh if run is None else jnp.maximum(run, sh)
                mt_buf[p, :, cols] = jnp.max(run, axis=0, keepdims=True)

        def numerators(p, s_buf, mt_buf, p_buf, al_buf):
            m_old = m_s[p]
            m_new = jnp.maximum(m_old, mt_buf[p])
            m_safe = jnp.where(m_new == neg_inf, 0.0, m_new)
            al_buf[p] = jnp.exp2(m_old - m_safe)
            m_s[p] = m_new
            for r0 in range(0, K_TILE, EXP_ROWS):
                d = (s_buf[p, r0:r0 + EXP_ROWS, :] - m_safe).astype(MXU_DTYPE)
                p_buf[p, r0:r0 + EXP_ROWS, :] = jnp.exp2(d)

        def weighted_values(p, j, p_buf, al_buf):
            g = (2 * p) // HEADS_PER_KV
            lhs = jnp.concatenate(
                [vt_ref[0, jnp.clip(j, 0, last_tile), g * HEAD_DIM:(g + 1) * HEAD_DIM, :], ones_rows],
                axis=0)
            pv = _dot(lhs, p_buf[p])
            acc_s[p] = al_buf[p] * acc_s[p] + pv[0:HEAD_DIM]
            l_s[p] = al_buf[p] * l_s[p] + pv[HEAD_DIM:HEAD_DIM + 1]

        def stage(j, s_next, mt_next, s_cur, mt_cur, p_cur, al_cur, p_prev, al_prev):
            selection_bias(j + 1)
            for p in range(N_PAIRS):
                logits(p, j + 1, s_next, mt_next)
                numerators(p, s_cur, mt_cur, p_cur, al_cur)
                weighted_values(p, j - 1, p_prev, al_prev)

        p1_s[...] = jnp.zeros(p1_s.shape, p1_s.dtype)
        al1_s[...] = jnp.ones(al1_s.shape, f32)
        selection_bias(0)
        for p in range(N_PAIRS):
            logits(p, 0, s0_s, mt0_s)

        def attend_two_tiles(jj, carry):
            j = 2 * jj
            stage(j, s1_s, mt1_s, s0_s, mt0_s, p0_s, al0_s, p1_s, al1_s)
            stage(j + 1, s0_s, mt0_s, s1_s, mt1_s, p1_s, al1_s, p0_s, al0_s)
            return carry

        n_two = (n_kt + 1) // 2
        lax.fori_loop(0, n_two, attend_two_tiles, 0)
        for p in range(N_PAIRS):
            weighted_values(p, 2 * n_two - 1, p1_s, al1_s)

    @pl.when(jnp.logical_not(has_ties))
    def _():
        attend(False)

    @pl.when(has_ties)
    def _():
        attend(True)

    for p in range(N_PAIRS):
        o = acc_s[p] / l_s[p]
        ot_s[(2 * p) * HEAD_DIM:(2 * p + 1) * HEAD_DIM, :] = o[:, 0:Q_TILE]
        ot_s[(2 * p + 1) * HEAD_DIM:(2 * p + 2) * HEAD_DIM, :] = o[:, Q_TILE:2 * Q_TILE]
    o_ref[0] = ot_s[...].T.astype(o_ref.dtype)


def _attn(qt, qit, wit, k, ki, vt, tri):
    B, nq, _, _ = qt.shape
    Tp = nq * Q_TILE
    nkt = Tp // K_TILE
    f32 = jnp.float32
    per_q = lambda rows: pl.BlockSpec((1, 1, rows, Q_TILE), lambda b, i: (b, i, 0, 0))
    once = pl.Buffered(1)
    return pl.pallas_call(
        _attn_kernel,
        grid=(B, nq),
        in_specs=[
            per_q(N_HEADS * HEAD_DIM),
            per_q(IDX_HEADS * IDX_DIM),
            per_q(IDX_HEADS),
            pl.BlockSpec((1, Tp, N_KV_HEADS * LANES), lambda b, i: (b, 0, 0), pipeline_mode=once),
            pl.BlockSpec((1, Tp, LANES), lambda b, i: (b, 0, 0), pipeline_mode=once),
            pl.BlockSpec((1, nkt, N_KV_HEADS * HEAD_DIM, K_TILE), lambda b, i: (b, 0, 0, 0),
                         pipeline_mode=once),
            pl.BlockSpec((K_TILE, K_TILE), lambda b, i: (0, 0), pipeline_mode=once),
        ],
        out_specs=pl.BlockSpec((1, Q_TILE, N_HEADS * HEAD_DIM), lambda b, i: (b, i, 0)),
        out_shape=jax.ShapeDtypeStruct((B, Tp, N_HEADS * HEAD_DIM), MXU_DTYPE),
        scratch_shapes=[
            pltpu.VMEM((-(-nkt // SWEEP_UNROLL) * SWEEP_UNROLL, K_TILE, Q_TILE), f32),
            pltpu.VMEM((LANES, IDX_HEADS * Q_TILE), MXU_DTYPE),
            pltpu.VMEM((1, IDX_HEADS * Q_TILE), f32),
            pltpu.VMEM((N_PAIRS, LANES, 2 * Q_TILE), MXU_DTYPE),
            pltpu.VMEM((N_PAIRS, 1, 2 * Q_TILE), f32),
            pltpu.VMEM((N_PAIRS, 1, 2 * Q_TILE), f32),
            pltpu.VMEM((N_PAIRS, HEAD_DIM, 2 * Q_TILE), f32),
            pltpu.VMEM((K_TILE, Q_TILE), f32),
            pltpu.VMEM((1, Q_TILE), f32),
            pltpu.VMEM((N_HEADS * HEAD_DIM, Q_TILE), f32),
            pltpu.VMEM((N_PAIRS, K_TILE, 2 * Q_TILE), f32),
            pltpu.VMEM((N_PAIRS, K_TILE, 2 * Q_TILE), f32),
            pltpu.VMEM((N_PAIRS, 1, 2 * Q_TILE), f32),
            pltpu.VMEM((N_PAIRS, 1, 2 * Q_TILE), f32),
            pltpu.VMEM((N_PAIRS, K_TILE, 2 * Q_TILE), MXU_DTYPE),
            pltpu.VMEM((N_PAIRS, K_TILE, 2 * Q_TILE), MXU_DTYPE),
            pltpu.VMEM((N_PAIRS, 1, 2 * Q_TILE), f32),
            pltpu.VMEM((N_PAIRS, 1, 2 * Q_TILE), f32),
        ],
        compiler_params=pltpu.CompilerParams(
            dimension_semantics=("arbitrary", "arbitrary"), vmem_limit_bytes=VMEM_LIMIT),
        name="attn",
    )(qt, qit, wit, k, ki, vt, tri)


def _mix_kernel(o_ref, u_ref, up_ref, ga_ref, gc_ref, h_ref, wa_ref, wc_ref, wo_ref,
                cw_ref, cb_ref, lg_ref, lb_ref, out_ref, ux_ref, sh_ref, cv_ref):
    f32 = jnp.float32
    i = pl.program_id(1)
    prev = up_ref[0].astype(f32)
    ux_ref[0:CONV_HALO, :] = jnp.where(i > 0, prev, jnp.zeros_like(prev))
    ux_ref[CONV_HALO:CONV_HALO + MIX_TILE, :] = u_ref[0].astype(f32)
    ext = CONV_HALO + MIX_TILE
    for c0 in range(0, D_MODEL, CONV_COLS):
        cols = slice(c0, c0 + CONV_COLS)
        for r in range(1, SUBLANES):
            sh_ref[r - 1, SUBLANES:ext, :] = ux_ref[SUBLANES - r:ext - r, cols]
        acc = jnp.broadcast_to(cb_ref[:, cols], (MIX_TILE, CONV_COLS))
        for j in range(CONV_WIDTH):
            a, r = divmod(CONV_WIDTH - 1 - j, SUBLANES)
            start = CONV_HALO - SUBLANES * a
            src = ux_ref[start:start + MIX_TILE, cols] if r == 0 else sh_ref[
                r - 1, start:start + MIX_TILE, :]
            acc = acc + cw_ref[j:j + 1, cols] * src
        cv_ref[:, cols] = acc
    acc = cv_ref[...]
    mu = jnp.mean(acc, axis=-1, keepdims=True)
    cen = acc - mu
    var = jnp.mean(cen * cen, axis=-1, keepdims=True)
    y = cen * lax.rsqrt(var + LN_EPS) * lg_ref[...] + lb_ref[...]
    y = y * _sigmoid(y)
    y_conv = _dot(y.astype(MXU_DTYPE), wc_ref[...])
    y_attn = _dot(o_ref[0], wa_ref[...])
    merged = ga_ref[0].astype(f32) * y_attn + gc_ref[0].astype(f32) * y_conv
    out_ref[0] = h_ref[0] + _dot(merged.astype(MXU_DTYPE), wo_ref[...])


def _mix(o, u, ga, gc, h, wa, wc, wo, cw, cb, lg, lb):
    B, Tp, D = h.shape
    row = pl.BlockSpec((1, MIX_TILE, D), lambda b, i: (b, i, 0))
    halo_blocks = MIX_TILE // CONV_HALO
    halo = pl.BlockSpec((1, CONV_HALO, D),
                        lambda b, i: (b, jnp.maximum(i * halo_blocks - 1, 0), 0))
    const = lambda shape: pl.BlockSpec(shape, lambda b, i: (0,) * len(shape),
                                       pipeline_mode=pl.Buffered(1))
    return pl.pallas_call(
        _mix_kernel,
        grid=(B, Tp // MIX_TILE),
        in_specs=[row, row, halo, row, row, row,
                  const((D, D)), const((D, D)), const((D, D)),
                  const((CONV_WIDTH, D)), const((1, D)), const((1, D)), const((1, D))],
        out_specs=row,
        out_shape=jax.ShapeDtypeStruct((B, Tp, D), jnp.float32),
        scratch_shapes=[
            pltpu.VMEM((CONV_HALO + MIX_TILE, D), jnp.float32),
            pltpu.VMEM((SUBLANES - 1, CONV_HALO + MIX_TILE, CONV_COLS), jnp.float32),
            pltpu.VMEM((MIX_TILE, D), jnp.float32),
        ],
        compiler_params=pltpu.CompilerParams(
            dimension_semantics=("arbitrary", "arbitrary"), vmem_limit_bytes=VMEM_LIMIT),
        name="mix",
    )(o, u, u, ga, gc, h, wa, wc, wo, cw, cb, lg, lb)


def _ffn_kernel(*refs, final_norm):
    if final_norm:
        h_ref, hp_ref, g_ref, wu_ref, cw_ref, cb_ref, wd_ref, fg_ref, out_ref, xn_ref, act_ref = refs
    else:
        h_ref, hp_ref, g_ref, wu_ref, cw_ref, cb_ref, wd_ref, out_ref, xn_ref, act_ref = refs
    i = pl.program_id(1)

    def norm(x):
        ms = jnp.mean(x * x, axis=-1, keepdims=True)
        return x * lax.rsqrt(ms + RMS_EPS) * g_ref[...]

    prev = norm(hp_ref[0])
    xn_ref[0:FFN_HALO, :] = jnp.where(i > 0, prev, jnp.zeros_like(prev)).astype(xn_ref.dtype)
    x = h_ref[0]
    xn_ref[FFN_HALO:FFN_HALO + FFN_TILE, :] = norm(x).astype(xn_ref.dtype)
    xn = xn_ref[...]

    def conv(c0):
        hcol = _dot(xn, wu_ref[:, c0:c0 + FFN_CHUNK])
        out = jnp.broadcast_to(cb_ref[:, c0:c0 + FFN_CHUNK], (FFN_TILE, FFN_CHUNK))
        for j in range(FFN_CONV_WIDTH):
            back = FFN_CONV_WIDTH - 1 - j
            shifted = hcol if back == 0 else pltpu.roll(hcol, back, 0)
            out = out + cw_ref[j:j + 1, c0:c0 + FFN_CHUNK] * shifted[FFN_HALO:FFN_HALO + FFN_TILE]
        return out

    for c0 in range(0, FFN_HIDDEN, FFN_CHUNK):
        gate = conv(c0)
        up = conv(FFN_HIDDEN + c0)
        act_ref[:, c0:c0 + FFN_CHUNK] = (gate * _sigmoid(gate) * up).astype(act_ref.dtype)
    y = x + _dot(act_ref[...], wd_ref[...])
    if final_norm:
        y = y * lax.rsqrt(jnp.mean(y * y, axis=-1, keepdims=True) + RMS_EPS) * fg_ref[...]
    out_ref[0] = y


def _ffn(h, g, wu, cw, cb, wd, final_g=None):
    B, Tp, D = h.shape
    final_norm = final_g is not None
    row = pl.BlockSpec((1, FFN_TILE, D), lambda b, i: (b, i, 0))
    halo_blocks = FFN_TILE // FFN_HALO
    halo = pl.BlockSpec((1, FFN_HALO, D),
                        lambda b, i: (b, jnp.maximum(i * halo_blocks - 1, 0), 0))
    const = lambda shape: pl.BlockSpec(shape, lambda b, i: (0,) * len(shape),
                                       pipeline_mode=pl.Buffered(1))
    operands = [h, h, g, wu, cw, cb, wd] + ([final_g] if final_norm else [])
    return pl.pallas_call(
        functools.partial(_ffn_kernel, final_norm=final_norm),
        grid=(B, Tp // FFN_TILE),
        in_specs=[row, halo, const((1, D)), const((D, 2 * FFN_HIDDEN)),
                  const((FFN_CONV_WIDTH, 2 * FFN_HIDDEN)), const((1, 2 * FFN_HIDDEN)),
                  const((FFN_HIDDEN, D))] + ([const((1, D))] if final_norm else []),
        out_specs=row,
        out_shape=jax.ShapeDtypeStruct((B, Tp, D), jnp.float32),
        scratch_shapes=[pltpu.VMEM((FFN_HALO + FFN_TILE, D), MXU_DTYPE),
                        pltpu.VMEM((FFN_TILE, FFN_HIDDEN), MXU_DTYPE)],
        compiler_params=pltpu.CompilerParams(
            dimension_semantics=("arbitrary", "arbitrary"), vmem_limit_bytes=VMEM_LIMIT),
        name="ffn",
    )(*operands)


def _rope_tables(Tp):
    half = ROT_DIM // 2
    pos = jnp.arange(Tp, dtype=jnp.float32)
    inv_freq = jnp.power(jnp.float32(ROPE_THETA),
                         -jnp.arange(0, ROT_DIM, 2, dtype=jnp.float32) / ROT_DIM)
    ang = pos[:, None] * inv_freq[None, :]
    cos, sin = jnp.cos(ang), jnp.sin(ang)
    zeros = lambda n: jnp.zeros((Tp, n), jnp.float32)
    cos_h = jnp.concatenate([cos, cos, jnp.ones((Tp, HEAD_DIM - ROT_DIM), jnp.float32)], axis=1)
    sina_h = jnp.concatenate([-sin, zeros(HEAD_DIM - half)], axis=1)
    sinb_h = jnp.concatenate([zeros(half), sin, zeros(HEAD_DIM - ROT_DIM)], axis=1)
    two = lambda t: jnp.concatenate([t, t], axis=1)
    return two(cos_h), two(sina_h), two(sinb_h)


def _pack_w_in(w):
    sizes = (N_HEADS * HEAD_DIM, N_KV_HEADS * HEAD_DIM, N_KV_HEADS * HEAD_DIM,
             IDX_HEADS * IDX_DIM, IDX_DIM, IDX_HEADS, 2 * D_MODEL, D_MODEL, D_MODEL)
    offs = [0]
    for s in sizes:
        offs.append(offs[-1] + s)
    wq, wk, wv, wqi, wki, wwi, wglu, wga, wgc = (w[:, offs[n]:offs[n + 1]] for n in range(9))

    def dup_heads(m):
        n = m.shape[1] // HEAD_DIM
        m = m.reshape(m.shape[0], n, 1, HEAD_DIM)
        return jnp.broadcast_to(m, (m.shape[0], n, 2, HEAD_DIM)).reshape(m.shape[0], n * LANES)

    pad_lanes = lambda m: jnp.pad(m, ((0, 0), (0, LANES - m.shape[1])))
    q_scale = (HEAD_DIM ** -0.5) * math.log2(math.e)
    cols = [wq * q_scale, wqi * (IDX_DIM ** -0.5), dup_heads(wk), pad_lanes(wki), wv,
            pad_lanes(wwi), wglu, wga, wgc]
    return jnp.concatenate(cols, axis=1).astype(MXU_DTYPE)


def kernel(x, meta_tokens, attn_norm_g, w_in, w_attn_out, conv_dw_w, conv_dw_b, conv_ln_g,
           conv_ln_b, w_conv_out, w_o, ffn_norm_g, w_up, ffn_dw_w, ffn_dw_b, w_down,
           final_norm_g):
    B, S, D = x.shape
    T = N_META + S
    Tp = -(-T // ROW_TILE) * ROW_TILE
    bf = MXU_DTYPE
    meta = jnp.broadcast_to(meta_tokens[None].astype(x.dtype), (B, N_META, D))
    h = jnp.concatenate([meta, x, jnp.zeros((B, Tp - T, D), x.dtype)], axis=1)
    cos, sina, sinb = _rope_tables(Tp)
    tri = (lax.broadcasted_iota(jnp.int32, (K_TILE, K_TILE), 1)
           < lax.broadcasted_iota(jnp.int32, (K_TILE, K_TILE), 0)).astype(bf)
    row = lambda v: v.reshape(1, -1)
    depth = w_in.shape[0]
    assert depth >= 1
    for l in range(depth):
        qt, qit, k, ki, vt, wit, u, ga, gc = _proj(h, row(attn_norm_g[l]), _pack_w_in(w_in[l]),
                                                   cos, sina, sinb)
        o = _attn(qt, qit, wit, k, ki, vt, tri)
        h = _mix(o, u, ga, gc, h, w_attn_out[l].astype(bf), w_conv_out[l].astype(bf),
                 w_o[l].astype(bf), conv_dw_w[l], row(conv_dw_b[l]), row(conv_ln_g[l]),
                 row(conv_ln_b[l]))
        h = _ffn(h, row(ffn_norm_g[l]), w_up[l].astype(bf), ffn_dw_w[l], row(ffn_dw_b[l]),
                 w_down[l].astype(bf), final_g=row(final_norm_g) if l == depth - 1 else None)
    return h[:, N_META:T]
```

```python
import functools
import math

import jax
import jax.numpy as jnp
from jax import lax
from jax.experimental import pallas as pl
from jax.experimental.pallas import tpu as pltpu

D_MODEL = 1024
N_META = 16
N_HEADS = 16
HEAD_DIM = 64
N_KV_HEADS = 4
HEADS_PER_KV = N_HEADS // N_KV_HEADS
N_PAIRS = N_HEADS // 2
IDX_HEADS = 8
IDX_DIM = 64
TOPK_MAX = 256
ROPE_THETA = 500000.0
ROT_DIM = HEAD_DIM // 4
CONV_WIDTH = 31
FFN_HIDDEN = 2816
FFN_CONV_WIDTH = 3
RMS_EPS = 1e-6
LN_EPS = 1e-5

LANES = 128
SUBLANES = 8
Q_TILE = 128
K_TILE = 256
ROW_TILE = 768
MIX_TILE = 384
FFN_TILE = 384
CONV_HALO = 32
CONV_COLS = 256
FFN_HALO = 16
FFN_CHUNK = 256
N_BISECT = 14
SCORE_UNROLL = 4
SWEEP_UNROLL = 4
EXP_ROWS = 32
LOGIT_BOUND = 100.0
DENOM_ROWS = 16
VMEM_LIMIT = 56 * 1024 * 1024
MXU_DTYPE = jnp.bfloat16

_C_Q = 0
_C_QI = _C_Q + N_HEADS * HEAD_DIM
_C_K = _C_QI + IDX_HEADS * IDX_DIM
_C_KI = _C_K + N_KV_HEADS * LANES
_C_V = _C_KI + LANES
_C_WI = _C_V + N_KV_HEADS * HEAD_DIM
_C_GLU = _C_WI + LANES
_C_GATE = _C_GLU + 2 * D_MODEL
_C_END = _C_GATE + 2 * D_MODEL


def _sigmoid(x):
    return 1.0 / (1.0 + jnp.exp(-x))


def _dot(a, b):
    return jnp.dot(a, b, preferred_element_type=jnp.float32)


def _proj_kernel(h_ref, g_ref, w_ref, cos_ref, sina_ref, sinb_ref,
                 qt_ref, qit_ref, k_ref, ki_ref, vt_ref, wit_ref, u_ref, ga_ref, gc_ref, nb_ref):
    x = h_ref[0]
    ms = jnp.mean(x * x, axis=-1, keepdims=True)
    xn = (x * lax.rsqrt(ms + RMS_EPS) * g_ref[...]).astype(MXU_DTYPE)
    cos = cos_ref[...]
    sina = sina_ref[...]
    sinb = sinb_ref[...]
    tiles = ROW_TILE // Q_TILE

    def rope(r):
        return (r * cos + pltpu.roll(r, LANES - ROT_DIM // 2, 1) * sina
                + pltpu.roll(r, ROT_DIM // 2, 1) * sinb)

    def proj(c0, width):
        return _dot(xn, w_ref[:, c0:c0 + width])

    def store_transposed(ref, row0, val):
        vt = val.T.astype(ref.dtype)
        for t in range(tiles):
            ref[0, t, row0:row0 + LANES, :] = vt[:, t * Q_TILE:(t + 1) * Q_TILE]

    q_bound = jnp.zeros((ROW_TILE, 1), jnp.float32)
    k_bound = jnp.zeros((ROW_TILE, 1), jnp.float32)
    for c0 in range(_C_Q, _C_QI, 2 * LANES):
        r = proj(c0, 2 * LANES)
        for s in range(2):
            roped = rope(r[:, s * LANES:(s + 1) * LANES])
            q_bound = jnp.maximum(q_bound, jnp.sum(roped * roped, axis=1, keepdims=True))
            store_transposed(qt_ref, c0 - _C_Q + s * LANES, roped)
    for c0 in range(_C_QI, _C_K, 2 * LANES):
        r = proj(c0, 2 * LANES)
        for s in range(2):
            store_transposed(qit_ref, c0 - _C_QI + s * LANES, rope(r[:, s * LANES:(s + 1) * LANES]))
    for c0 in range(_C_K, _C_KI, 2 * LANES):
        r = proj(c0, 2 * LANES)
        for s in range(2):
            col = c0 - _C_K + s * LANES
            roped = rope(r[:, s * LANES:(s + 1) * LANES])
            k_bound = jnp.maximum(k_bound, 0.5 * jnp.sum(roped * roped, axis=1, keepdims=True))
            k_ref[0, :, col:col + LANES] = roped.astype(k_ref.dtype)
    ki_ref[0] = rope(proj(_C_KI, LANES)).astype(ki_ref.dtype)
    upper = lax.broadcasted_iota(jnp.int32, (SUBLANES, LANES), 0) < SUBLANES // 2
    nb_ref[0, 0] = jnp.where(upper, jnp.max(q_bound), jnp.max(k_bound))
    v = proj(_C_V, 2 * LANES)
    for s in range(2):
        vt = v[:, s * LANES:(s + 1) * LANES].T.astype(vt_ref.dtype)
        for t in range(ROW_TILE // K_TILE):
            vt_ref[0, t, s * LANES:(s + 1) * LANES, :] = vt[:, t * K_TILE:(t + 1) * K_TILE]
    wit = (proj(_C_WI, LANES) * (IDX_HEADS ** -0.5)).T
    for t in range(tiles):
        wit_ref[0, t] = wit[0:IDX_HEADS, t * Q_TILE:(t + 1) * Q_TILE]
    for c0 in range(0, D_MODEL, 2 * LANES):
        a = proj(_C_GLU + c0, 2 * LANES)
        b = proj(_C_GLU + D_MODEL + c0, 2 * LANES)
        u_ref[0, :, c0:c0 + 2 * LANES] = (a * _sigmoid(b)).astype(u_ref.dtype)
    for c0 in range(0, D_MODEL, 2 * LANES):
        ga_ref[0, :, c0:c0 + 2 * LANES] = _sigmoid(proj(_C_GATE + c0, 2 * LANES)).astype(ga_ref.dtype)
        gc_ref[0, :, c0:c0 + 2 * LANES] = _sigmoid(
            proj(_C_GATE + D_MODEL + c0, 2 * LANES)).astype(gc_ref.dtype)


def _proj(h, g, w, cos, sina, sinb):
    B, Tp, D = h.shape
    nt = Tp // ROW_TILE
    nq = Tp // Q_TILE
    nk = Tp // K_TILE
    qpt = ROW_TILE // Q_TILE
    kpt = ROW_TILE // K_TILE
    bf = MXU_DTYPE
    row = lambda width: pl.BlockSpec((1, ROW_TILE, width), lambda b, i: (b, i, 0))
    tab = pl.BlockSpec((ROW_TILE, LANES), lambda b, i: (i, 0))
    per_qtile = lambda rows: pl.BlockSpec((1, qpt, rows, Q_TILE), lambda b, i: (b, i, 0, 0))
    out_shape = (
        jax.ShapeDtypeStruct((B, nq, N_HEADS * HEAD_DIM, Q_TILE), bf),
        jax.ShapeDtypeStruct((B, nq, IDX_HEADS * IDX_DIM, Q_TILE), bf),
        jax.ShapeDtypeStruct((B, Tp, N_KV_HEADS * LANES), bf),
        jax.ShapeDtypeStruct((B, Tp, LANES), bf),
        jax.ShapeDtypeStruct((B, nk, N_KV_HEADS * HEAD_DIM, K_TILE), bf),
        jax.ShapeDtypeStruct((B, nq, IDX_HEADS, Q_TILE), jnp.float32),
        jax.ShapeDtypeStruct((B, Tp, D), bf),
        jax.ShapeDtypeStruct((B, Tp, D), bf),
        jax.ShapeDtypeStruct((B, Tp, D), bf),
        jax.ShapeDtypeStruct((B, nt, SUBLANES, LANES), jnp.float32),
    )
    out_specs = (
        per_qtile(N_HEADS * HEAD_DIM),
        per_qtile(IDX_HEADS * IDX_DIM),
        row(N_KV_HEADS * LANES),
        row(LANES),
        pl.BlockSpec((1, kpt, N_KV_HEADS * HEAD_DIM, K_TILE), lambda b, i: (b, i, 0, 0)),
        per_qtile(IDX_HEADS),
        row(D), row(D), row(D),
        pl.BlockSpec((1, 1, SUBLANES, LANES), lambda b, i: (b, i, 0, 0)),
    )
    return pl.pallas_call(
        _proj_kernel,
        grid=(B, nt),
        in_specs=[
            row(D),
            pl.BlockSpec((1, D), lambda b, i: (0, 0)),
            pl.BlockSpec((D, _C_END), lambda b, i: (0, 0), pipeline_mode=pl.Buffered(1)),
            tab, tab, tab,
        ],
        out_specs=out_specs,
        out_shape=out_shape,
        compiler_params=pltpu.CompilerParams(
            dimension_semantics=("arbitrary", "arbitrary"), vmem_limit_bytes=VMEM_LIMIT),
        name="proj",
    )(h, g, w, cos, sina, sinb)


def _fold_sublanes(x, op):
    parts = [x[r:r + SUBLANES] for r in range(0, x.shape[0], SUBLANES)]
    while len(parts) > 1:
        parts = [op(parts[n], parts[n + 1]) if n + 1 < len(parts) else parts[n]
                 for n in range(0, len(parts), 2)]
    return parts[0]


def _attn_kernel(qt_ref, qit_ref, wit_ref, k_ref, ki_ref, vt_ref, tri_ref, nb_ref, o_ref,
                 sc_ref, qi_s, wb_s, qp_s, m_s, l_s, acc_s, bias_s, tie_s, ot_s,
                 s0_s, s1_s, mt0_s, mt1_s, p0_s, p1_s, al0_s, al1_s, sel_s):
    f32 = jnp.float32
    i = pl.program_id(1)
    n_kt = (i * Q_TILE + Q_TILE + K_TILE - 1) // K_TILE
    qpos = i * Q_TILE + lax.broadcasted_iota(jnp.int32, (1, Q_TILE), 1)
    neg_inf = f32(-jnp.inf)
    pos_inf = f32(jnp.inf)
    lane_vec = lambda v: jnp.full((1, Q_TILE), v, f32)

    qi_s[...] = jnp.zeros(qi_s.shape, qi_s.dtype)
    for hd in range(IDX_HEADS):
        qi_s[0:IDX_DIM, hd * Q_TILE:(hd + 1) * Q_TILE] = qit_ref[0, 0, hd * IDX_DIM:(hd + 1) * IDX_DIM, :]
        wb_s[:, hd * Q_TILE:(hd + 1) * Q_TILE] = wit_ref[0, 0, hd:hd + 1, :]

    n_score = (n_kt + SCORE_UNROLL - 1) // SCORE_UNROLL
    n_sweep = (n_kt + SWEEP_UNROLL - 1) // SWEEP_UNROLL
    last_row0 = k_ref.shape[1] - K_TILE

    def score_tile(j, carry):
        rmax, rmin = carry
        row0 = pl.multiple_of(j * K_TILE, K_TILE)
        read0 = pl.multiple_of(jnp.minimum(row0, last_row0), K_TILE)
        lg = _dot(ki_ref[0, pl.ds(read0, K_TILE), :], qi_s[...])
        r = jnp.maximum(lg, 0.0) * wb_s[...]
        s = r[:, 0:Q_TILE]
        for hd in range(1, IDX_HEADS):
            s = s + r[:, hd * Q_TILE:(hd + 1) * Q_TILE]
        kpos = row0 + lax.broadcasted_iota(jnp.int32, (K_TILE, 1), 0)
        causal = kpos <= qpos
        masked = jnp.where(causal, s, neg_inf)
        sc_ref[j] = masked
        rmax = jnp.maximum(rmax, _fold_sublanes(masked, jnp.maximum))
        rmin = jnp.minimum(rmin, _fold_sublanes(jnp.where(causal, s, pos_inf), jnp.minimum))
        return rmax, rmin

    def score_group(jj, carry):
        for u in range(SCORE_UNROLL):
            carry = score_tile(jj * SCORE_UNROLL + u, carry)
        return carry

    rmax, rmin = lax.fori_loop(
        0, n_score, score_group,
        (jnp.full((SUBLANES, Q_TILE), neg_inf, f32), jnp.full((SUBLANES, Q_TILE), pos_inf, f32)))
    rmax = jnp.max(rmax, axis=0, keepdims=True)
    rmin = jnp.min(rmin, axis=0, keepdims=True)

    def fill_tile(j, carry):
        sc_ref[j] = jnp.full((K_TILE, Q_TILE), neg_inf, f32)
        return carry

    lax.fori_loop(n_score * SCORE_UNROLL, n_sweep * SWEEP_UNROLL, fill_tile, 0)

    def count_where(pred_fn):
        def body(jj, acc):
            for u in range(SWEEP_UNROLL):
                t = sc_ref[jj * SWEEP_UNROLL + u]
                acc = acc + _fold_sublanes(jnp.where(pred_fn(t), 1.0, 0.0), jnp.add)
            return acc
        acc = lax.fori_loop(0, n_sweep, body, jnp.zeros((SUBLANES, Q_TILE), f32))
        return jnp.sum(acc, axis=0, keepdims=True)

    def max_below(bound):
        def body(jj, acc):
            for u in range(SWEEP_UNROLL):
                t = sc_ref[jj * SWEEP_UNROLL + u]
                acc = jnp.maximum(
                    acc, _fold_sublanes(jnp.where(t < bound, t, neg_inf), jnp.maximum))
            return acc
        acc = lax.fori_loop(0, n_sweep, body, jnp.full((SUBLANES, Q_TILE), neg_inf, f32))
        return jnp.max(acc, axis=0, keepdims=True)

    k_sel = f32(TOPK_MAX)
    active = qpos >= TOPK_MAX

    def bisect(_, carry):
        lo, hi, c_hi, strict = carry
        mid = lo + 0.5 * (hi - lo)
        c = count_where(lambda t: t >= mid)
        ge = c >= k_sel
        return (jnp.where(ge, mid, lo), jnp.where(ge, hi, mid),
                jnp.where(ge, c_hi, c), jnp.where(ge, strict, 1.0))

    lo, hi, c_hi, strict = lax.fori_loop(
        0, N_BISECT, bisect, (rmin, rmax, lane_vec(0.0), lane_vec(0.0)))

    def walk_cond(carry):
        return jnp.min(carry[0]) < 0.5

    def walk_body(carry):
        done, bound, cnt, thr, need, n_eq = carry
        x = max_below(bound)
        e = count_where(lambda t: t == x)
        finished = jnp.logical_or(cnt + e >= k_sel, x == neg_inf)
        newly = jnp.logical_and(done < 0.5, finished)
        thr = jnp.where(newly, x, thr)
        need = jnp.where(newly, k_sel - cnt, need)
        n_eq = jnp.where(newly, e, n_eq)
        done = jnp.where(newly, 1.0, done)
        still = done < 0.5
        return (done, jnp.where(still, x, bound), jnp.where(still, cnt + e, cnt), thr, need, n_eq)

    is_strict = strict > 0.5
    _, _, _, thr, need, n_eq = lax.while_loop(
        walk_cond, walk_body,
        (jnp.where(active, 0.0, 1.0), jnp.where(is_strict, hi, pos_inf),
         jnp.where(is_strict, c_hi, 0.0), lane_vec(0.0), lane_vec(0.0), lane_vec(0.0)))
    thr = jnp.where(active, thr, rmin)
    need = jnp.where(active, need, f32(2 ** 30))
    has_ties = jnp.max(jnp.where(jnp.logical_and(active, n_eq > need), 1.0, 0.0)) > 0.5

    qp_s[...] = jnp.zeros(qp_s.shape, qp_s.dtype)
    for p in range(N_PAIRS):
        qp_s[p, 0:HEAD_DIM, 0:Q_TILE] = qt_ref[0, 0, (2 * p) * HEAD_DIM:(2 * p + 1) * HEAD_DIM, :]
        qp_s[p, HEAD_DIM:LANES, Q_TILE:2 * Q_TILE] = qt_ref[
            0, 0, (2 * p + 1) * HEAD_DIM:(2 * p + 2) * HEAD_DIM, :]
    m_s[...] = jnp.full(m_s.shape, neg_inf, f32)
    l_s[...] = jnp.zeros(l_s.shape, f32)
    acc_s[...] = jnp.zeros(acc_s.shape, f32)
    tie_s[...] = jnp.zeros(tie_s.shape, f32)

    last_tile = n_kt - 1
    ones_rows = jnp.ones((DENOM_ROWS, K_TILE), MXU_DTYPE)

    def attend(tie_aware):
        def selection_bias(j):
            thr_j = jnp.where(j <= last_tile, thr, pos_inf)
            t = sc_ref[jnp.minimum(j, last_tile)]

            if not tie_aware:
                bias_s[...] = jnp.where(t >= thr_j, 0.0, neg_inf)
            else:
                eq = t == thr_j
                eqf = jnp.where(eq, 1.0, 0.0)
                rank = tie_s[...] + _dot(tri_ref[...], eqf.astype(MXU_DTYPE))
                sel = jnp.logical_or(t > thr_j, jnp.logical_and(eq, rank < need))
                bias_s[...] = jnp.where(sel, 0.0, neg_inf)
                tie_s[...] = tie_s[...] + jnp.sum(eqf, axis=0, keepdims=True)

        def logits(p, j, s_buf, mt_buf):
            g = (2 * p) // HEADS_PER_KV
            row0 = pl.multiple_of(jnp.minimum(j, last_tile) * K_TILE, K_TILE)
            s = _dot(k_ref[0, pl.ds(row0, K_TILE), g * LANES:(g + 1) * LANES], qp_s[p])
            for half in range(2):
                cols = slice(half * Q_TILE, (half + 1) * Q_TILE)
                run = None
                for r0 in range(0, K_TILE, EXP_ROWS):
                    sh = s[r0:r0 + EXP_ROWS, cols] + bias_s[r0:r0 + EXP_ROWS, :]
                    s_buf[p, r0:r0 + EXP_ROWS, cols] = sh
                    run = sh if run is None else jnp.maximum(run, sh)
                mt_buf[p, :, cols] = jnp.max(run, axis=0, keepdims=True)

        def numerators(p, s_buf, mt_buf, p_buf, al_buf):
            m_old = m_s[p]
            m_new = jnp.maximum(m_old, mt_buf[p])
            m_safe = jnp.where(m_new == neg_inf, 0.0, m_new)
            al_buf[p] = jnp.exp2(m_old - m_safe)
            m_s[p] = m_new
            for r0 in range(0, K_TILE, EXP_ROWS):
                d = (s_buf[p, r0:r0 + EXP_ROWS, :] - m_safe).astype(MXU_DTYPE)
                p_buf[p, r0:r0 + EXP_ROWS, :] = jnp.exp2(d)

        def weighted_values(p, j, p_buf, al_buf):
            g = (2 * p) // HEADS_PER_KV
            lhs = jnp.concatenate(
                [vt_ref[0, jnp.clip(j, 0, last_tile), g * HEAD_DIM:(g + 1) * HEAD_DIM, :], ones_rows],
                axis=0)
            pv = _dot(lhs, p_buf[p])
            acc_s[p] = al_buf[p] * acc_s[p] + pv[0:HEAD_DIM]
            l_s[p] = al_buf[p] * l_s[p] + pv[HEAD_DIM:HEAD_DIM + 1]

        def stage(j, s_next, mt_next, s_cur, mt_cur, p_cur, al_cur, p_prev, al_prev):
            selection_bias(j + 1)
            for p in range(N_PAIRS):
                logits(p, j + 1, s_next, mt_next)
                numerators(p, s_cur, mt_cur, p_cur, al_cur)
                weighted_values(p, j - 1, p_prev, al_prev)

        p1_s[...] = jnp.zeros(p1_s.shape, p1_s.dtype)
        al1_s[...] = jnp.ones(al1_s.shape, f32)
        selection_bias(0)
        for p in range(N_PAIRS):
            logits(p, 0, s0_s, mt0_s)

        def attend_two_tiles(jj, carry):
            j = 2 * jj
            stage(j, s1_s, mt1_s, s0_s, mt0_s, p0_s, al0_s, p1_s, al1_s)
            stage(j + 1, s0_s, mt0_s, s1_s, mt1_s, p1_s, al1_s, p0_s, al0_s)
            return carry

        n_two = (n_kt + 1) // 2
        lax.fori_loop(0, n_two, attend_two_tiles, 0)
        for p in range(N_PAIRS):
            weighted_values(p, 2 * n_two - 1, p1_s, al1_s)

    def attend_bounded(tie_aware):
        def selection(j):
            thr_j = jnp.where(j <= last_tile, thr, pos_inf)
            t = sc_ref[jnp.minimum(j, last_tile)]
            if not tie_aware:
                sel = t >= thr_j
            else:
                eq = t == thr_j
                eqf = jnp.where(eq, 1.0, 0.0)
                rank = tie_s[...] + _dot(tri_ref[...], eqf.astype(MXU_DTYPE))
                sel = jnp.logical_or(t > thr_j, jnp.logical_and(eq, rank < need))
                tie_s[...] = tie_s[...] + jnp.sum(eqf, axis=0, keepdims=True)
            sel_s[...] = jnp.where(sel, 1.0, 0.0).astype(sel_s.dtype)

        def numerators(p, j, p_buf):
            g = (2 * p) // HEADS_PER_KV
            row0 = pl.multiple_of(jnp.minimum(j, last_tile) * K_TILE, K_TILE)
            s = _dot(k_ref[0, pl.ds(row0, K_TILE), g * LANES:(g + 1) * LANES], qp_s[p])
            for half in range(2):
                cols = slice(half * Q_TILE, (half + 1) * Q_TILE)
                for r0 in range(0, K_TILE, EXP_ROWS):
                    e = jnp.exp2(s[r0:r0 + EXP_ROWS, cols]).astype(p_buf.dtype)
                    p_buf[p, r0:r0 + EXP_ROWS, cols] = e * sel_s[r0:r0 + EXP_ROWS, :]

        def weighted_values(p, j, p_buf):
            g = (2 * p) // HEADS_PER_KV
            lhs = jnp.concatenate(
                [vt_ref[0, jnp.clip(j, 0, last_tile), g * HEAD_DIM:(g + 1) * HEAD_DIM, :], ones_rows],
                axis=0)
            pv = _dot(lhs, p_buf[p])
            acc_s[p] = acc_s[p] + pv[0:HEAD_DIM]
            l_s[p] = l_s[p] + pv[HEAD_DIM:HEAD_DIM + 1]

        def stage(j, p_cur, p_prev):
            selection(j)
            for p in range(N_PAIRS):
                numerators(p, j, p_cur)
                weighted_values(p, j - 1, p_prev)

        p1_s[...] = jnp.zeros(p1_s.shape, p1_s.dtype)

        def attend_two_tiles(jj, carry):
            stage(2 * jj, p0_s, p1_s)
            stage(2 * jj + 1, p1_s, p0_s)
            return carry

        n_two = (n_kt + 1) // 2
        lax.fori_loop(0, n_two, attend_two_tiles, 0)
        for p in range(N_PAIRS):
            weighted_values(p, 2 * n_two - 1, p1_s)

    bounds = nb_ref[0]
    q_bound = jnp.max(bounds[:, 0:SUBLANES // 2, :])
    k_bound = jnp.max(bounds[:, SUBLANES // 2:SUBLANES, :])
    bounded = q_bound * k_bound <= LOGIT_BOUND * LOGIT_BOUND

    @pl.when(jnp.logical_and(bounded, jnp.logical_not(has_ties)))
    def _():
        attend_bounded(False)

    @pl.when(jnp.logical_and(bounded, has_ties))
    def _():
        attend_bounded(True)

    @pl.when(jnp.logical_not(bounded))
    def _():
        attend(True)

    for p in range(N_PAIRS):
        o = acc_s[p] / l_s[p]
        ot_s[(2 * p) * HEAD_DIM:(2 * p + 1) * HEAD_DIM, :] = o[:, 0:Q_TILE]
        ot_s[(2 * p + 1) * HEAD_DIM:(2 * p + 2) * HEAD_DIM, :] = o[:, Q_TILE:2 * Q_TILE]
    o_ref[0] = ot_s[...].T.astype(o_ref.dtype)


def _attn(qt, qit, wit, k, ki, vt, tri, nb):
    B, nq, _, _ = qt.shape
    Tp = nq * Q_TILE
    nkt = Tp // K_TILE
    f32 = jnp.float32
    per_q = lambda rows: pl.BlockSpec((1, 1, rows, Q_TILE), lambda b, i: (b, i, 0, 0))
    once = pl.Buffered(1)
    return pl.pallas_call(
        _attn_kernel,
        grid=(B, nq),
        in_specs=[
            per_q(N_HEADS * HEAD_DIM),
            per_q(IDX_HEADS * IDX_DIM),
            per_q(IDX_HEADS),
            pl.BlockSpec((1, Tp, N_KV_HEADS * LANES), lambda b, i: (b, 0, 0), pipeline_mode=once),
            pl.BlockSpec((1, Tp, LANES), lambda b, i: (b, 0, 0), pipeline_mode=once),
            pl.BlockSpec((1, nkt, N_KV_HEADS * HEAD_DIM, K_TILE), lambda b, i: (b, 0, 0, 0),
                         pipeline_mode=once),
            pl.BlockSpec((K_TILE, K_TILE), lambda b, i: (0, 0), pipeline_mode=once),
            pl.BlockSpec((1, nb.shape[1], SUBLANES, LANES), lambda b, i: (b, 0, 0, 0),
                         pipeline_mode=once),
        ],
        out_specs=pl.BlockSpec((1, Q_TILE, N_HEADS * HEAD_DIM), lambda b, i: (b, i, 0)),
        out_shape=jax.ShapeDtypeStruct((B, Tp, N_HEADS * HEAD_DIM), MXU_DTYPE),
        scratch_shapes=[
            pltpu.VMEM((-(-nkt // SWEEP_UNROLL) * SWEEP_UNROLL, K_TILE, Q_TILE), f32),
            pltpu.VMEM((LANES, IDX_HEADS * Q_TILE), MXU_DTYPE),
            pltpu.VMEM((1, IDX_HEADS * Q_TILE), f32),
            pltpu.VMEM((N_PAIRS, LANES, 2 * Q_TILE), MXU_DTYPE),
            pltpu.VMEM((N_PAIRS, 1, 2 * Q_TILE), f32),
            pltpu.VMEM((N_PAIRS, 1, 2 * Q_TILE), f32),
            pltpu.VMEM((N_PAIRS, HEAD_DIM, 2 * Q_TILE), f32),
            pltpu.VMEM((K_TILE, Q_TILE), f32),
            pltpu.VMEM((1, Q_TILE), f32),
            pltpu.VMEM((N_HEADS * HEAD_DIM, Q_TILE), f32),
            pltpu.VMEM((N_PAIRS, K_TILE, 2 * Q_TILE), f32),
            pltpu.VMEM((N_PAIRS, K_TILE, 2 * Q_TILE), f32),
            pltpu.VMEM((N_PAIRS, 1, 2 * Q_TILE), f32),
            pltpu.VMEM((N_PAIRS, 1, 2 * Q_TILE), f32),
            pltpu.VMEM((N_PAIRS, K_TILE, 2 * Q_TILE), MXU_DTYPE),
            pltpu.VMEM((N_PAIRS, K_TILE, 2 * Q_TILE), MXU_DTYPE),
            pltpu.VMEM((N_PAIRS, 1, 2 * Q_TILE), f32),
            pltpu.VMEM((N_PAIRS, 1, 2 * Q_TILE), f32),
            pltpu.VMEM((K_TILE, Q_TILE), MXU_DTYPE),
        ],
        compiler_params=pltpu.CompilerParams(
            dimension_semantics=("arbitrary", "arbitrary"), vmem_limit_bytes=VMEM_LIMIT),
        name="attn",
    )(qt, qit, wit, k, ki, vt, tri, nb)


def _mix_kernel(o_ref, u_ref, up_ref, ga_ref, gc_ref, h_ref, wa_ref, wc_ref, wo_ref,
                cw_ref, cb_ref, lg_ref, lb_ref, out_ref, ux_ref, sh_ref, cv_ref):
    f32 = jnp.float32
    i = pl.program_id(1)
    prev = up_ref[0].astype(f32)
    ux_ref[0:CONV_HALO, :] = jnp.where(i > 0, prev, jnp.zeros_like(prev))
    ux_ref[CONV_HALO:CONV_HALO + MIX_TILE, :] = u_ref[0].astype(f32)
    ext = CONV_HALO + MIX_TILE
    for c0 in range(0, D_MODEL, CONV_COLS):
        cols = slice(c0, c0 + CONV_COLS)
        for r in range(1, SUBLANES):
            sh_ref[r - 1, SUBLANES:ext, :] = ux_ref[SUBLANES - r:ext - r, cols]
        acc = jnp.broadcast_to(cb_ref[:, cols], (MIX_TILE, CONV_COLS))
        for j in range(CONV_WIDTH):
            a, r = divmod(CONV_WIDTH - 1 - j, SUBLANES)
            start = CONV_HALO - SUBLANES * a
            src = ux_ref[start:start + MIX_TILE, cols] if r == 0 else sh_ref[
                r - 1, start:start + MIX_TILE, :]
            acc = acc + cw_ref[j:j + 1, cols] * src
        cv_ref[:, cols] = acc
    acc = cv_ref[...]
    mu = jnp.mean(acc, axis=-1, keepdims=True)
    cen = acc - mu
    var = jnp.mean(cen * cen, axis=-1, keepdims=True)
    y = cen * lax.rsqrt(var + LN_EPS) * lg_ref[...] + lb_ref[...]
    y = y * _sigmoid(y)
    y_conv = _dot(y.astype(MXU_DTYPE), wc_ref[...])
    y_attn = _dot(o_ref[0], wa_ref[...])
    merged = ga_ref[0].astype(f32) * y_attn + gc_ref[0].astype(f32) * y_conv
    out_ref[0] = h_ref[0] + _dot(merged.astype(MXU_DTYPE), wo_ref[...])


def _mix(o, u, ga, gc, h, wa, wc, wo, cw, cb, lg, lb):
    B, Tp, D = h.shape
    row = pl.BlockSpec((1, MIX_TILE, D), lambda b, i: (b, i, 0))
    halo_blocks = MIX_TILE // CONV_HALO
    halo = pl.BlockSpec((1, CONV_HALO, D),
                        lambda b, i: (b, jnp.maximum(i * halo_blocks - 1, 0), 0))
    const = lambda shape: pl.BlockSpec(shape, lambda b, i: (0,) * len(shape),
                                       pipeline_mode=pl.Buffered(1))
    return pl.pallas_call(
        _mix_kernel,
        grid=(B, Tp // MIX_TILE),
        in_specs=[row, row, halo, row, row, row,
                  const((D, D)), const((D, D)), const((D, D)),
                  const((CONV_WIDTH, D)), const((1, D)), const((1, D)), const((1, D))],
        out_specs=row,
        out_shape=jax.ShapeDtypeStruct((B, Tp, D), jnp.float32),
        scratch_shapes=[
            pltpu.VMEM((CONV_HALO + MIX_TILE, D), jnp.float32),
            pltpu.VMEM((SUBLANES - 1, CONV_HALO + MIX_TILE, CONV_COLS), jnp.float32),
            pltpu.VMEM((MIX_TILE, D), jnp.float32),
        ],
        compiler_params=pltpu.CompilerParams(
            dimension_semantics=("arbitrary", "arbitrary"), vmem_limit_bytes=VMEM_LIMIT),
        name="mix",
    )(o, u, u, ga, gc, h, wa, wc, wo, cw, cb, lg, lb)


def _ffn_kernel(*refs, final_norm):
    if final_norm:
        h_ref, hp_ref, g_ref, wu_ref, cw_ref, cb_ref, wd_ref, fg_ref, out_ref, xn_ref, act_ref = refs
    else:
        h_ref, hp_ref, g_ref, wu_ref, cw_ref, cb_ref, wd_ref, out_ref, xn_ref, act_ref = refs
    i = pl.program_id(1)

    def norm(x):
        ms = jnp.mean(x * x, axis=-1, keepdims=True)
        return x * lax.rsqrt(ms + RMS_EPS) * g_ref[...]

    prev = norm(hp_ref[0])
    xn_ref[0:FFN_HALO, :] = jnp.where(i > 0, prev, jnp.zeros_like(prev)).astype(xn_ref.dtype)
    x = h_ref[0]
    xn_ref[FFN_HALO:FFN_HALO + FFN_TILE, :] = norm(x).astype(xn_ref.dtype)
    xn = xn_ref[...]

    def conv(c0):
        hcol = _dot(xn, wu_ref[:, c0:c0 + FFN_CHUNK])
        out = jnp.broadcast_to(cb_ref[:, c0:c0 + FFN_CHUNK], (FFN_TILE, FFN_CHUNK))
        for j in range(FFN_CONV_WIDTH):
            back = FFN_CONV_WIDTH - 1 - j
            shifted = hcol if back == 0 else pltpu.roll(hcol, back, 0)
            out = out + cw_ref[j:j + 1, c0:c0 + FFN_CHUNK] * shifted[FFN_HALO:FFN_HALO + FFN_TILE]
        return out

    for c0 in range(0, FFN_HIDDEN, FFN_CHUNK):
        gate = conv(c0)
        up = conv(FFN_HIDDEN + c0)
        act_ref[:, c0:c0 + FFN_CHUNK] = (gate * _sigmoid(gate) * up).astype(act_ref.dtype)
    y = x + _dot(act_ref[...], wd_ref[...])
    if final_norm:
        y = y * lax.rsqrt(jnp.mean(y * y, axis=-1, keepdims=True) + RMS_EPS) * fg_ref[...]
    out_ref[0] = y


def _ffn(h, g, wu, cw, cb, wd, final_g=None):
    B, Tp, D = h.shape
    final_norm = final_g is not None
    row = pl.BlockSpec((1, FFN_TILE, D), lambda b, i: (b, i, 0))
    halo_blocks = FFN_TILE // FFN_HALO
    halo = pl.BlockSpec((1, FFN_HALO, D),
                        lambda b, i: (b, jnp.maximum(i * halo_blocks - 1, 0), 0))
    const = lambda shape: pl.BlockSpec(shape, lambda b, i: (0,) * len(shape),
                                       pipeline_mode=pl.Buffered(1))
    operands = [h, h, g, wu, cw, cb, wd] + ([final_g] if final_norm else [])
    return pl.pallas_call(
        functools.partial(_ffn_kernel, final_norm=final_norm),
        grid=(B, Tp // FFN_TILE),
        in_specs=[row, halo, const((1, D)), const((D, 2 * FFN_HIDDEN)),
                  const((FFN_CONV_WIDTH, 2 * FFN_HIDDEN)), const((1, 2 * FFN_HIDDEN)),
                  const((FFN_HIDDEN, D))] + ([const((1, D))] if final_norm else []),
        out_specs=row,
        out_shape=jax.ShapeDtypeStruct((B, Tp, D), jnp.float32),
        scratch_shapes=[pltpu.VMEM((FFN_HALO + FFN_TILE, D), MXU_DTYPE),
                        pltpu.VMEM((FFN_TILE, FFN_HIDDEN), MXU_DTYPE)],
        compiler_params=pltpu.CompilerParams(
            dimension_semantics=("arbitrary", "arbitrary"), vmem_limit_bytes=VMEM_LIMIT),
        name="ffn",
    )(*operands)


def _rope_tables(Tp):
    half = ROT_DIM // 2
    pos = jnp.arange(Tp, dtype=jnp.float32)
    inv_freq = jnp.power(jnp.float32(ROPE_THETA),
                         -jnp.arange(0, ROT_DIM, 2, dtype=jnp.float32) / ROT_DIM)
    ang = pos[:, None] * inv_freq[None, :]
    cos, sin = jnp.cos(ang), jnp.sin(ang)
    zeros = lambda n: jnp.zeros((Tp, n), jnp.float32)
    cos_h = jnp.concatenate([cos, cos, jnp.ones((Tp, HEAD_DIM - ROT_DIM), jnp.float32)], axis=1)
    sina_h = jnp.concatenate([-sin, zeros(HEAD_DIM - half)], axis=1)
    sinb_h = jnp.concatenate([zeros(half), sin, zeros(HEAD_DIM - ROT_DIM)], axis=1)
    two = lambda t: jnp.concatenate([t, t], axis=1)
    return two(cos_h), two(sina_h), two(sinb_h)


def _pack_w_in(w):
    sizes = (N_HEADS * HEAD_DIM, N_KV_HEADS * HEAD_DIM, N_KV_HEADS * HEAD_DIM,
             IDX_HEADS * IDX_DIM, IDX_DIM, IDX_HEADS, 2 * D_MODEL, D_MODEL, D_MODEL)
    offs = [0]
    for s in sizes:
        offs.append(offs[-1] + s)
    wq, wk, wv, wqi, wki, wwi, wglu, wga, wgc = (w[:, offs[n]:offs[n + 1]] for n in range(9))

    def dup_heads(m):
        n = m.shape[1] // HEAD_DIM
        m = m.reshape(m.shape[0], n, 1, HEAD_DIM)
        return jnp.broadcast_to(m, (m.shape[0], n, 2, HEAD_DIM)).reshape(m.shape[0], n * LANES)

    pad_lanes = lambda m: jnp.pad(m, ((0, 0), (0, LANES - m.shape[1])))
    q_scale = (HEAD_DIM ** -0.5) * math.log2(math.e)
    cols = [wq * q_scale, wqi * (IDX_DIM ** -0.5), dup_heads(wk), pad_lanes(wki), wv,
            pad_lanes(wwi), wglu, wga, wgc]
    return jnp.concatenate(cols, axis=1).astype(MXU_DTYPE)


def kernel(x, meta_tokens, attn_norm_g, w_in, w_attn_out, conv_dw_w, conv_dw_b, conv_ln_g,
           conv_ln_b, w_conv_out, w_o, ffn_norm_g, w_up, ffn_dw_w, ffn_dw_b, w_down,
           final_norm_g):
    B, S, D = x.shape
    T = N_META + S
    Tp = -(-T // ROW_TILE) * ROW_TILE
    bf = MXU_DTYPE
    meta = jnp.broadcast_to(meta_tokens[None].astype(x.dtype), (B, N_META, D))
    h = jnp.concatenate([meta, x, jnp.zeros((B, Tp - T, D), x.dtype)], axis=1)
    cos, sina, sinb = _rope_tables(Tp)
    tri = (lax.broadcasted_iota(jnp.int32, (K_TILE, K_TILE), 1)
           < lax.broadcasted_iota(jnp.int32, (K_TILE, K_TILE), 0)).astype(bf)
    row = lambda v: v.reshape(1, -1)
    depth = w_in.shape[0]
    assert depth >= 1
    for l in range(depth):
        qt, qit, k, ki, vt, wit, u, ga, gc, nb = _proj(h, row(attn_norm_g[l]), _pack_w_in(w_in[l]),
                                                       cos, sina, sinb)
        o = _attn(qt, qit, wit, k, ki, vt, tri, nb)
        h = _mix(o, u, ga, gc, h, w_attn_out[l].astype(bf), w_conv_out[l].astype(bf),
                 w_o[l].astype(bf), conv_dw_w[l], row(conv_dw_b[l]), row(conv_ln_g[l]),
                 row(conv_ln_b[l]))
        h = _ffn(h, row(ffn_norm_g[l]), w_up[l].astype(bf), ffn_dw_w[l], row(ffn_dw_b[l]),
                 w_down[l].astype(bf), final_g=row(final_norm_g) if l == depth - 1 else None)
    return h[:, N_META:T]
```

```python
import functools
import math

import jax
import jax.numpy as jnp
from jax import lax
from jax.experimental import pallas as pl
from jax.experimental.pallas import tpu as pltpu

D_MODEL = 1024
N_META = 16
N_HEADS = 16
HEAD_DIM = 64
N_KV_HEADS = 4
HEADS_PER_KV = N_HEADS // N_KV_HEADS
N_PAIRS = N_HEADS // 2
IDX_HEADS = 8
IDX_DIM = 64
TOPK_MAX = 256
ROPE_THETA = 500000.0
ROT_DIM = HEAD_DIM // 4
CONV_WIDTH = 31
FFN_HIDDEN = 2816
FFN_CONV_WIDTH = 3
RMS_EPS = 1e-6
LN_EPS = 1e-5

LANES = 128
SUBLANES = 8
Q_TILE = 128
K_TILE = 256
ROW_TILE = 768
MIX_TILE = 384
FFN_TILE = 384
CONV_HALO = 32
CONV_COLS = 256
FFN_HALO = 16
FFN_CHUNK = 256
N_BISECT = 13
SCORE_UNROLL = 4
SWEEP_UNROLL = 4
EXP_ROWS = 32
LOGIT_BOUND = 100.0
DENOM_ROWS = 16
VMEM_LIMIT = 56 * 1024 * 1024
MXU_DTYPE = jnp.bfloat16

_C_Q = 0
_C_QI = _C_Q + N_HEADS * HEAD_DIM
_C_K = _C_QI + IDX_HEADS * IDX_DIM
_C_KI = _C_K + N_KV_HEADS * LANES
_C_V = _C_KI + LANES
_C_WI = _C_V + N_KV_HEADS * HEAD_DIM
_C_GLU = _C_WI + LANES
_C_GATE = _C_GLU + 2 * D_MODEL
_C_END = _C_GATE + 2 * D_MODEL


def _sigmoid(x):
    return 1.0 / (1.0 + jnp.exp(-x))


def _dot(a, b):
    return jnp.dot(a, b, preferred_element_type=jnp.float32)


def _proj_kernel(h_ref, g_ref, w_ref, cos_ref, sina_ref, sinb_ref,
                 qt_ref, qit_ref, k_ref, ki_ref, vt_ref, wit_ref, u_ref, ga_ref, gc_ref, nb_ref):
    x = h_ref[0]
    ms = jnp.mean(x * x, axis=-1, keepdims=True)
    xn = (x * lax.rsqrt(ms + RMS_EPS) * g_ref[...]).astype(MXU_DTYPE)
    cos = cos_ref[...]
    sina = sina_ref[...]
    sinb = sinb_ref[...]
    tiles = ROW_TILE // Q_TILE

    def rope(r):
        return (r * cos + pltpu.roll(r, LANES - ROT_DIM // 2, 1) * sina
                + pltpu.roll(r, ROT_DIM // 2, 1) * sinb)

    def proj(c0, width):
        return _dot(xn, w_ref[:, c0:c0 + width])

    def store_transposed(ref, row0, val):
        vt = val.T.astype(ref.dtype)
        for t in range(tiles):
            ref[0, t, row0:row0 + LANES, :] = vt[:, t * Q_TILE:(t + 1) * Q_TILE]

    q_bound = jnp.zeros((ROW_TILE, 1), jnp.float32)
    k_bound = jnp.zeros((ROW_TILE, 1), jnp.float32)
    for c0 in range(_C_Q, _C_QI, 2 * LANES):
        r = proj(c0, 2 * LANES)
        for s in range(2):
            roped = rope(r[:, s * LANES:(s + 1) * LANES])
            q_bound = jnp.maximum(q_bound, jnp.sum(roped * roped, axis=1, keepdims=True))
            store_transposed(qt_ref, c0 - _C_Q + s * LANES, roped)
    for c0 in range(_C_QI, _C_K, 2 * LANES):
        r = proj(c0, 2 * LANES)
        for s in range(2):
            store_transposed(qit_ref, c0 - _C_QI + s * LANES, rope(r[:, s * LANES:(s + 1) * LANES]))
    for c0 in range(_C_K, _C_KI, 2 * LANES):
        r = proj(c0, 2 * LANES)
        for s in range(2):
            col = c0 - _C_K + s * LANES
            roped = rope(r[:, s * LANES:(s + 1) * LANES])
            k_bound = jnp.maximum(k_bound, 0.5 * jnp.sum(roped * roped, axis=1, keepdims=True))
            k_ref[0, :, col:col + LANES] = roped.astype(k_ref.dtype)
    ki_ref[0] = rope(proj(_C_KI, LANES)).astype(ki_ref.dtype)
    upper = lax.broadcasted_iota(jnp.int32, (SUBLANES, LANES), 0) < SUBLANES // 2
    nb_ref[0, 0] = jnp.where(upper, jnp.max(q_bound), jnp.max(k_bound))
    v = proj(_C_V, 2 * LANES)
    for s in range(2):
        vt = v[:, s * LANES:(s + 1) * LANES].T.astype(vt_ref.dtype)
        for t in range(ROW_TILE // K_TILE):
            vt_ref[0, t, s * LANES:(s + 1) * LANES, :] = vt[:, t * K_TILE:(t + 1) * K_TILE]
    wit = (proj(_C_WI, LANES) * (IDX_HEADS ** -0.5)).T
    for t in range(tiles):
        wit_ref[0, t] = wit[0:IDX_HEADS, t * Q_TILE:(t + 1) * Q_TILE]
    for c0 in range(0, D_MODEL, 2 * LANES):
        a = proj(_C_GLU + c0, 2 * LANES)
        b = proj(_C_GLU + D_MODEL + c0, 2 * LANES)
        u_ref[0, :, c0:c0 + 2 * LANES] = (a * _sigmoid(b)).astype(u_ref.dtype)
    for c0 in range(0, D_MODEL, 2 * LANES):
        ga_ref[0, :, c0:c0 + 2 * LANES] = _sigmoid(proj(_C_GATE + c0, 2 * LANES)).astype(ga_ref.dtype)
        gc_ref[0, :, c0:c0 + 2 * LANES] = _sigmoid(
            proj(_C_GATE + D_MODEL + c0, 2 * LANES)).astype(gc_ref.dtype)


def _proj(h, g, w, cos, sina, sinb):
    B, Tp, D = h.shape
    nt = Tp // ROW_TILE
    nq = Tp // Q_TILE
    nk = Tp // K_TILE
    qpt = ROW_TILE // Q_TILE
    kpt = ROW_TILE // K_TILE
    bf = MXU_DTYPE
    row = lambda width: pl.BlockSpec((1, ROW_TILE, width), lambda b, i: (b, i, 0))
    tab = pl.BlockSpec((ROW_TILE, LANES), lambda b, i: (i, 0))
    per_qtile = lambda rows: pl.BlockSpec((1, qpt, rows, Q_TILE), lambda b, i: (b, i, 0, 0))
    out_shape = (
        jax.ShapeDtypeStruct((B, nq, N_HEADS * HEAD_DIM, Q_TILE), bf),
        jax.ShapeDtypeStruct((B, nq, IDX_HEADS * IDX_DIM, Q_TILE), bf),
        jax.ShapeDtypeStruct((B, Tp, N_KV_HEADS * LANES), bf),
        jax.ShapeDtypeStruct((B, Tp, LANES), bf),
        jax.ShapeDtypeStruct((B, nk, N_KV_HEADS * HEAD_DIM, K_TILE), bf),
        jax.ShapeDtypeStruct((B, nq, IDX_HEADS, Q_TILE), jnp.float32),
        jax.ShapeDtypeStruct((B, Tp, D), bf),
        jax.ShapeDtypeStruct((B, Tp, D), bf),
        jax.ShapeDtypeStruct((B, Tp, D), bf),
        jax.ShapeDtypeStruct((B, nt, SUBLANES, LANES), jnp.float32),
    )
    out_specs = (
        per_qtile(N_HEADS * HEAD_DIM),
        per_qtile(IDX_HEADS * IDX_DIM),
        row(N_KV_HEADS * LANES),
        row(LANES),
        pl.BlockSpec((1, kpt, N_KV_HEADS * HEAD_DIM, K_TILE), lambda b, i: (b, i, 0, 0)),
        per_qtile(IDX_HEADS),
        row(D), row(D), row(D),
        pl.BlockSpec((1, 1, SUBLANES, LANES), lambda b, i: (b, i, 0, 0)),
    )
    return pl.pallas_call(
        _proj_kernel,
        grid=(B, nt),
        in_specs=[
            row(D),
            pl.BlockSpec((1, D), lambda b, i: (0, 0)),
            pl.BlockSpec((D, _C_END), lambda b, i: (0, 0), pipeline_mode=pl.Buffered(1)),
            tab, tab, tab,
        ],
        out_specs=out_specs,
        out_shape=out_shape,
        compiler_params=pltpu.CompilerParams(
            dimension_semantics=("arbitrary", "arbitrary"), vmem_limit_bytes=VMEM_LIMIT),
        name="proj",
    )(h, g, w, cos, sina, sinb)


def _fold_sublanes(x, op):
    parts = [x[r:r + SUBLANES] for r in range(0, x.shape[0], SUBLANES)]
    while len(parts) > 1:
        parts = [op(parts[n], parts[n + 1]) if n + 1 < len(parts) else parts[n]
                 for n in range(0, len(parts), 2)]
    return parts[0]


def _attn_kernel(qt_ref, qit_ref, wit_ref, k_ref, ki_ref, vt_ref, tri_ref, nb_ref, o_ref,
                 sc_ref, qi_s, wb_s, qp_s, m_s, l_s, acc_s, bias_s, tie_s, ot_s,
                 s0_s, s1_s, mt0_s, mt1_s, p0_s, p1_s, al0_s, al1_s, sel_s):
    f32 = jnp.float32
    i = pl.program_id(1)
    n_kt = (i * Q_TILE + Q_TILE + K_TILE - 1) // K_TILE
    qpos = i * Q_TILE + lax.broadcasted_iota(jnp.int32, (1, Q_TILE), 1)
    neg_inf = f32(-jnp.inf)
    pos_inf = f32(jnp.inf)
    lane_vec = lambda v: jnp.full((1, Q_TILE), v, f32)

    qi_s[...] = jnp.zeros(qi_s.shape, qi_s.dtype)
    for hd in range(IDX_HEADS):
        qi_s[0:IDX_DIM, hd * Q_TILE:(hd + 1) * Q_TILE] = qit_ref[0, 0, hd * IDX_DIM:(hd + 1) * IDX_DIM, :]
        wb_s[:, hd * Q_TILE:(hd + 1) * Q_TILE] = wit_ref[0, 0, hd:hd + 1, :]

    n_score = (n_kt + SCORE_UNROLL - 1) // SCORE_UNROLL
    n_sweep = (n_kt + SWEEP_UNROLL - 1) // SWEEP_UNROLL
    last_row0 = k_ref.shape[1] - K_TILE

    def score_tile(j, carry):
        rmax, rmin, n_pos = carry
        row0 = pl.multiple_of(j * K_TILE, K_TILE)
        read0 = pl.multiple_of(jnp.minimum(row0, last_row0), K_TILE)
        lg = _dot(ki_ref[0, pl.ds(read0, K_TILE), :], qi_s[...])
        r = jnp.maximum(lg, 0.0) * wb_s[...]
        s = r[:, 0:Q_TILE]
        for hd in range(1, IDX_HEADS):
            s = s + r[:, hd * Q_TILE:(hd + 1) * Q_TILE]
        kpos = row0 + lax.broadcasted_iota(jnp.int32, (K_TILE, 1), 0)
        causal = kpos <= qpos
        masked = jnp.where(causal, s, neg_inf)
        sc_ref[j] = masked
        rmax = jnp.maximum(rmax, _fold_sublanes(masked, jnp.maximum))
        rmin = jnp.minimum(rmin, _fold_sublanes(jnp.where(causal, s, pos_inf), jnp.minimum))
        n_pos = n_pos + _fold_sublanes(jnp.where(masked >= 0.0, 1.0, 0.0), jnp.add)
        return rmax, rmin, n_pos

    def score_group(jj, carry):
        for u in range(SCORE_UNROLL):
            carry = score_tile(jj * SCORE_UNROLL + u, carry)
        return carry

    rmax, rmin, n_pos = lax.fori_loop(
        0, n_score, score_group,
        (jnp.full((SUBLANES, Q_TILE), neg_inf, f32), jnp.full((SUBLANES, Q_TILE), pos_inf, f32),
         jnp.zeros((SUBLANES, Q_TILE), f32)))
    rmax = jnp.max(rmax, axis=0, keepdims=True)
    rmin = jnp.min(rmin, axis=0, keepdims=True)
    n_pos = jnp.sum(n_pos, axis=0, keepdims=True)

    def fill_tile(j, carry):
        sc_ref[j] = jnp.full((K_TILE, Q_TILE), neg_inf, f32)
        return carry

    lax.fori_loop(n_score * SCORE_UNROLL, n_sweep * SWEEP_UNROLL, fill_tile, 0)

    def count_where(pred_fn):
        def body(jj, acc):
            for u in range(SWEEP_UNROLL):
                t = sc_ref[jj * SWEEP_UNROLL + u]
                acc = acc + _fold_sublanes(jnp.where(pred_fn(t), 1.0, 0.0), jnp.add)
            return acc
        acc = lax.fori_loop(0, n_sweep, body, jnp.zeros((SUBLANES, Q_TILE), f32))
        return jnp.sum(acc, axis=0, keepdims=True)

    def max_below(bound):
        def body(jj, acc):
            for u in range(SWEEP_UNROLL):
                t = sc_ref[jj * SWEEP_UNROLL + u]
                acc = jnp.maximum(
                    acc, _fold_sublanes(jnp.where(t < bound, t, neg_inf), jnp.maximum))
            return acc
        acc = lax.fori_loop(0, n_sweep, body, jnp.full((SUBLANES, Q_TILE), neg_inf, f32))
        return jnp.max(acc, axis=0, keepdims=True)

    k_sel = f32(TOPK_MAX)
    active = qpos >= TOPK_MAX

    def bisect(_, carry):
        lo, hi, c_hi, strict = carry
        mid = lo + 0.5 * (hi - lo)
        c = count_where(lambda t: t >= mid)
        ge = c >= k_sel
        return (jnp.where(ge, mid, lo), jnp.where(ge, hi, mid),
                jnp.where(ge, c_hi, c), jnp.where(ge, strict, 1.0))

    zero_low = n_pos >= k_sel
    zero_high = jnp.logical_and(jnp.logical_not(zero_low), rmax > 0.0)
    lo, hi, c_hi, strict = lax.fori_loop(
        0, N_BISECT, bisect,
        (jnp.where(zero_low, jnp.maximum(rmin, 0.0), rmin), jnp.where(zero_high, 0.0, rmax),
         jnp.where(zero_high, n_pos, 0.0), jnp.where(zero_high, 1.0, 0.0)))

    def walk_cond(carry):
        return jnp.min(carry[0]) < 0.5

    def walk_body(carry):
        done, bound, cnt, thr, need, n_eq = carry
        x = max_below(bound)
        e = count_where(lambda t: t == x)
        finished = jnp.logical_or(cnt + e >= k_sel, x == neg_inf)
        newly = jnp.logical_and(done < 0.5, finished)
        thr = jnp.where(newly, x, thr)
        need = jnp.where(newly, k_sel - cnt, need)
        n_eq = jnp.where(newly, e, n_eq)
        done = jnp.where(newly, 1.0, done)
        still = done < 0.5
        return (done, jnp.where(still, x, bound), jnp.where(still, cnt + e, cnt), thr, need, n_eq)

    is_strict = strict > 0.5
    _, _, _, thr, need, n_eq = lax.while_loop(
        walk_cond, walk_body,
        (jnp.where(active, 0.0, 1.0), jnp.where(is_strict, hi, pos_inf),
         jnp.where(is_strict, c_hi, 0.0), lane_vec(0.0), lane_vec(0.0), lane_vec(0.0)))
    thr = jnp.where(active, thr, rmin)
    need = jnp.where(active, need, f32(2 ** 30))
    has_ties = jnp.max(jnp.where(jnp.logical_and(active, n_eq > need), 1.0, 0.0)) > 0.5

    qp_s[...] = jnp.zeros(qp_s.shape, qp_s.dtype)
    for p in range(N_PAIRS):
        qp_s[p, 0:HEAD_DIM, 0:Q_TILE] = qt_ref[0, 0, (2 * p) * HEAD_DIM:(2 * p + 1) * HEAD_DIM, :]
        qp_s[p, HEAD_DIM:LANES, Q_TILE:2 * Q_TILE] = qt_ref[
            0, 0, (2 * p + 1) * HEAD_DIM:(2 * p + 2) * HEAD_DIM, :]
    m_s[...] = jnp.full(m_s.shape, neg_inf, f32)
    l_s[...] = jnp.zeros(l_s.shape, f32)
    acc_s[...] = jnp.zeros(acc_s.shape, f32)
    tie_s[...] = jnp.zeros(tie_s.shape, f32)

    last_tile = n_kt - 1
    ones_rows = jnp.ones((DENOM_ROWS, K_TILE), MXU_DTYPE)

    def attend(tie_aware):
        def selection_bias(j):
            thr_j = jnp.where(j <= last_tile, thr, pos_inf)
            t = sc_ref[jnp.minimum(j, last_tile)]

            if not tie_aware:
                bias_s[...] = jnp.where(t >= thr_j, 0.0, neg_inf)
            else:
                eq = t == thr_j
                eqf = jnp.where(eq, 1.0, 0.0)
                rank = tie_s[...] + _dot(tri_ref[...], eqf.astype(MXU_DTYPE))
                sel = jnp.logical_or(t > thr_j, jnp.logical_and(eq, rank < need))
                bias_s[...] = jnp.where(sel, 0.0, neg_inf)
                tie_s[...] = tie_s[...] + jnp.sum(eqf, axis=0, keepdims=True)

        def logits(p, j, s_buf, mt_buf):
            g = (2 * p) // HEADS_PER_KV
            row0 = pl.multiple_of(jnp.minimum(j, last_tile) * K_TILE, K_TILE)
            s = _dot(k_ref[0, pl.ds(row0, K_TILE), g * LANES:(g + 1) * LANES], qp_s[p])
            for half in range(2):
                cols = slice(half * Q_TILE, (half + 1) * Q_TILE)
                run = None
                for r0 in range(0, K_TILE, EXP_ROWS):
                    sh = s[r0:r0 + EXP_ROWS, cols] + bias_s[r0:r0 + EXP_ROWS, :]
                    s_buf[p, r0:r0 + EXP_ROWS, cols] = sh
                    run = sh if run is None else jnp.maximum(run, sh)
                mt_buf[p, :, cols] = jnp.max(run, axis=0, keepdims=True)

        def numerators(p, s_buf, mt_buf, p_buf, al_buf):
            m_old = m_s[p]
            m_new = jnp.maximum(m_old, mt_buf[p])
            m_safe = jnp.where(m_new == neg_inf, 0.0, m_new)
            al_buf[p] = jnp.exp2(m_old - m_safe)
            m_s[p] = m_new
            for r0 in range(0, K_TILE, EXP_ROWS):
                d = (s_buf[p, r0:r0 + EXP_ROWS, :] - m_safe).astype(MXU_DTYPE)
                p_buf[p, r0:r0 + EXP_ROWS, :] = jnp.exp2(d)

        def weighted_values(p, j, p_buf, al_buf):
            g = (2 * p) // HEADS_PER_KV
            lhs = jnp.concatenate(
                [vt_ref[0, jnp.clip(j, 0, last_tile), g * HEAD_DIM:(g + 1) * HEAD_DIM, :], ones_rows],
                axis=0)
            pv = _dot(lhs, p_buf[p])
            acc_s[p] = al_buf[p] * acc_s[p] + pv[0:HEAD_DIM]
            l_s[p] = al_buf[p] * l_s[p] + pv[HEAD_DIM:HEAD_DIM + 1]

        def stage(j, s_next, mt_next, s_cur, mt_cur, p_cur, al_cur, p_prev, al_prev):
            selection_bias(j + 1)
            for p in range(N_PAIRS):
                logits(p, j + 1, s_next, mt_next)
                numerators(p, s_cur, mt_cur, p_cur, al_cur)
                weighted_values(p, j - 1, p_prev, al_prev)

        p1_s[...] = jnp.zeros(p1_s.shape, p1_s.dtype)
        al1_s[...] = jnp.ones(al1_s.shape, f32)
        selection_bias(0)
        for p in range(N_PAIRS):
            logits(p, 0, s0_s, mt0_s)

        def attend_two_tiles(jj, carry):
            j = 2 * jj
            stage(j, s1_s, mt1_s, s0_s, mt0_s, p0_s, al0_s, p1_s, al1_s)
            stage(j + 1, s0_s, mt0_s, s1_s, mt1_s, p1_s, al1_s, p0_s, al0_s)
            return carry

        n_two = (n_kt + 1) // 2
        lax.fori_loop(0, n_two, attend_two_tiles, 0)
        for p in range(N_PAIRS):
            weighted_values(p, 2 * n_two - 1, p1_s, al1_s)

    def attend_bounded(tie_aware):
        def selection(j):
            thr_j = jnp.where(j <= last_tile, thr, pos_inf)
            t = sc_ref[jnp.minimum(j, last_tile)]
            if not tie_aware:
                sel = t >= thr_j
            else:
                eq = t == thr_j
                eqf = jnp.where(eq, 1.0, 0.0)
                rank = tie_s[...] + _dot(tri_ref[...], eqf.astype(MXU_DTYPE))
                sel = jnp.logical_or(t > thr_j, jnp.logical_and(eq, rank < need))
                tie_s[...] = tie_s[...] + jnp.sum(eqf, axis=0, keepdims=True)
            sel_s[...] = jnp.where(sel, 1.0, 0.0).astype(sel_s.dtype)

        def numerators(p, j, p_buf):
            g = (2 * p) // HEADS_PER_KV
            row0 = pl.multiple_of(jnp.minimum(j, last_tile) * K_TILE, K_TILE)
            s = _dot(k_ref[0, pl.ds(row0, K_TILE), g * LANES:(g + 1) * LANES], qp_s[p])
            for half in range(2):
                cols = slice(half * Q_TILE, (half + 1) * Q_TILE)
                run = None
                for r0 in range(0, K_TILE, EXP_ROWS):
                    e = jnp.exp2(s[r0:r0 + EXP_ROWS, cols]).astype(p_buf.dtype)
                    e = e * sel_s[r0:r0 + EXP_ROWS, :]
                    p_buf[p, r0:r0 + EXP_ROWS, cols] = e
                    run = e.astype(f32) if run is None else run + e.astype(f32)
                l_s[p, :, cols] = l_s[p, :, cols] + jnp.sum(run, axis=0, keepdims=True)

        def weighted_values(p, j, p_buf):
            g = (2 * p) // HEADS_PER_KV
            vt = vt_ref[0, jnp.clip(j, 0, last_tile), g * HEAD_DIM:(g + 1) * HEAD_DIM, :]
            acc_s[p] = acc_s[p] + _dot(vt, p_buf[p])

        def stage(j, p_cur, p_prev):
            selection(j)
            for p in range(N_PAIRS):
                numerators(p, j, p_cur)
                weighted_values(p, j - 1, p_prev)

        p1_s[...] = jnp.zeros(p1_s.shape, p1_s.dtype)

        def attend_two_tiles(jj, carry):
            stage(2 * jj, p0_s, p1_s)
            stage(2 * jj + 1, p1_s, p0_s)
            return carry

        n_two = (n_kt + 1) // 2
        lax.fori_loop(0, n_two, attend_two_tiles, 0)
        for p in range(N_PAIRS):
            weighted_values(p, 2 * n_two - 1, p1_s)

    bounds = nb_ref[0]
    q_bound = jnp.max(bounds[:, 0:SUBLANES // 2, :])
    k_bound = jnp.max(bounds[:, SUBLANES // 2:SUBLANES, :])
    bounded = q_bound * k_bound <= LOGIT_BOUND * LOGIT_BOUND

    @pl.when(jnp.logical_and(bounded, jnp.logical_not(has_ties)))
    def _():
        attend_bounded(False)

    @pl.when(jnp.logical_and(bounded, has_ties))
    def _():
        attend_bounded(True)

    @pl.when(jnp.logical_not(bounded))
    def _():
        attend(True)

    for p in range(N_PAIRS):
        o = acc_s[p] / l_s[p]
        ot_s[(2 * p) * HEAD_DIM:(2 * p + 1) * HEAD_DIM, :] = o[:, 0:Q_TILE]
        ot_s[(2 * p + 1) * HEAD_DIM:(2 * p + 2) * HEAD_DIM, :] = o[:, Q_TILE:2 * Q_TILE]
    o_ref[0] = ot_s[...].T.astype(o_ref.dtype)


def _attn(qt, qit, wit, k, ki, vt, tri, nb):
    B, nq, _, _ = qt.shape
    Tp = nq * Q_TILE
    nkt = Tp // K_TILE
    f32 = jnp.float32
    per_q = lambda rows: pl.BlockSpec((1, 1, rows, Q_TILE), lambda b, i: (b, i, 0, 0))
    once = pl.Buffered(1)
    return pl.pallas_call(
        _attn_kernel,
        grid=(B, nq),
        in_specs=[
            per_q(N_HEADS * HEAD_DIM),
            per_q(IDX_HEADS * IDX_DIM),
            per_q(IDX_HEADS),
            pl.BlockSpec((1, Tp, N_KV_HEADS * LANES), lambda b, i: (b, 0, 0), pipeline_mode=once),
            pl.BlockSpec((1, Tp, LANES), lambda b, i: (b, 0, 0), pipeline_mode=once),
            pl.BlockSpec((1, nkt, N_KV_HEADS * HEAD_DIM, K_TILE), lambda b, i: (b, 0, 0, 0),
                         pipeline_mode=once),
            pl.BlockSpec((K_TILE, K_TILE), lambda b, i: (0, 0), pipeline_mode=once),
            pl.BlockSpec((1, nb.shape[1], SUBLANES, LANES), lambda b, i: (b, 0, 0, 0),
                         pipeline_mode=once),
        ],
        out_specs=pl.BlockSpec((1, Q_TILE, N_HEADS * HEAD_DIM), lambda b, i: (b, i, 0)),
        out_shape=jax.ShapeDtypeStruct((B, Tp, N_HEADS * HEAD_DIM), MXU_DTYPE),
        scratch_shapes=[
            pltpu.VMEM((-(-nkt // SWEEP_UNROLL) * SWEEP_UNROLL, K_TILE, Q_TILE), f32),
            pltpu.VMEM((LANES, IDX_HEADS * Q_TILE), MXU_DTYPE),
            pltpu.VMEM((1, IDX_HEADS * Q_TILE), f32),
            pltpu.VMEM((N_PAIRS, LANES, 2 * Q_TILE), MXU_DTYPE),
            pltpu.VMEM((N_PAIRS, 1, 2 * Q_TILE), f32),
            pltpu.VMEM((N_PAIRS, 1, 2 * Q_TILE), f32),
            pltpu.VMEM((N_PAIRS, HEAD_DIM, 2 * Q_TILE), f32),
            pltpu.VMEM((K_TILE, Q_TILE), f32),
            pltpu.VMEM((1, Q_TILE), f32),
            pltpu.VMEM((N_HEADS * HEAD_DIM, Q_TILE), f32),
            pltpu.VMEM((N_PAIRS, K_TILE, 2 * Q_TILE), f32),
            pltpu.VMEM((N_PAIRS, K_TILE, 2 * Q_TILE), f32),
            pltpu.VMEM((N_PAIRS, 1, 2 * Q_TILE), f32),
            pltpu.VMEM((N_PAIRS, 1, 2 * Q_TILE), f32),
            pltpu.VMEM((N_PAIRS, K_TILE, 2 * Q_TILE), MXU_DTYPE),
            pltpu.VMEM((N_PAIRS, K_TILE, 2 * Q_TILE), MXU_DTYPE),
            pltpu.VMEM((N_PAIRS, 1, 2 * Q_TILE), f32),
            pltpu.VMEM((N_PAIRS, 1, 2 * Q_TILE), f32),
            pltpu.VMEM((K_TILE, Q_TILE), MXU_DTYPE),
        ],
        compiler_params=pltpu.CompilerParams(
            dimension_semantics=("arbitrary", "arbitrary"), vmem_limit_bytes=VMEM_LIMIT),
        name="attn",
    )(qt, qit, wit, k, ki, vt, tri, nb)


def _mix_kernel(o_ref, u_ref, up_ref, ga_ref, gc_ref, h_ref, wa_ref, wc_ref, wo_ref,
                cw_ref, cb_ref, lg_ref, lb_ref, out_ref, ux_ref, sh_ref, cv_ref):
    f32 = jnp.float32
    i = pl.program_id(1)
    prev = up_ref[0].astype(f32)
    ux_ref[0:CONV_HALO, :] = jnp.where(i > 0, prev, jnp.zeros_like(prev))
    ux_ref[CONV_HALO:CONV_HALO + MIX_TILE, :] = u_ref[0].astype(f32)
    ext = CONV_HALO + MIX_TILE
    for c0 in range(0, D_MODEL, CONV_COLS):
        cols = slice(c0, c0 + CONV_COLS)
        for r in range(1, SUBLANES):
            sh_ref[r - 1, SUBLANES:ext, :] = ux_ref[SUBLANES - r:ext - r, cols]
        acc = jnp.broadcast_to(cb_ref[:, cols], (MIX_TILE, CONV_COLS))
        for j in range(CONV_WIDTH):
            a, r = divmod(CONV_WIDTH - 1 - j, SUBLANES)
            start = CONV_HALO - SUBLANES * a
            src = ux_ref[start:start + MIX_TILE, cols] if r == 0 else sh_ref[
                r - 1, start:start + MIX_TILE, :]
            acc = acc + cw_ref[j:j + 1, cols] * src
        cv_ref[:, cols] = acc
    acc = cv_ref[...]
    mu = jnp.mean(acc, axis=-1, keepdims=True)
    cen = acc - mu
    var = jnp.mean(cen * cen, axis=-1, keepdims=True)
    y = cen * lax.rsqrt(var + LN_EPS) * lg_ref[...] + lb_ref[...]
    y = y * _sigmoid(y)
    y_conv = _dot(y.astype(MXU_DTYPE), wc_ref[...])
    y_attn = _dot(o_ref[0], wa_ref[...])
    merged = ga_ref[0].astype(f32) * y_attn + gc_ref[0].astype(f32) * y_conv
    out_ref[0] = h_ref[0] + _dot(merged.astype(MXU_DTYPE), wo_ref[...])


def _mix(o, u, ga, gc, h, wa, wc, wo, cw, cb, lg, lb):
    B, Tp, D = h.shape
    row = pl.BlockSpec((1, MIX_TILE, D), lambda b, i: (b, i, 0))
    halo_blocks = MIX_TILE // CONV_HALO
    halo = pl.BlockSpec((1, CONV_HALO, D),
                        lambda b, i: (b, jnp.maximum(i * halo_blocks - 1, 0), 0))
    const = lambda shape: pl.BlockSpec(shape, lambda b, i: (0,) * len(shape),
                                       pipeline_mode=pl.Buffered(1))
    return pl.pallas_call(
        _mix_kernel,
        grid=(B, Tp // MIX_TILE),
        in_specs=[row, row, halo, row, row, row,
                  const((D, D)), const((D, D)), const((D, D)),
                  const((CONV_WIDTH, D)), const((1, D)), const((1, D)), const((1, D))],
        out_specs=row,
        out_shape=jax.ShapeDtypeStruct((B, Tp, D), jnp.float32),
        scratch_shapes=[
            pltpu.VMEM((CONV_HALO + MIX_TILE, D), jnp.float32),
            pltpu.VMEM((SUBLANES - 1, CONV_HALO + MIX_TILE, CONV_COLS), jnp.float32),
            pltpu.VMEM((MIX_TILE, D), jnp.float32),
        ],
        compiler_params=pltpu.CompilerParams(
            dimension_semantics=("arbitrary", "arbitrary"), vmem_limit_bytes=VMEM_LIMIT),
        name="mix",
    )(o, u, u, ga, gc, h, wa, wc, wo, cw, cb, lg, lb)


def _ffn_kernel(*refs, final_norm):
    if final_norm:
        h_ref, hp_ref, g_ref, wu_ref, cw_ref, cb_ref, wd_ref, fg_ref, out_ref, xn_ref, act_ref = refs
    else:
        h_ref, hp_ref, g_ref, wu_ref, cw_ref, cb_ref, wd_ref, out_ref, xn_ref, act_ref = refs
    i = pl.program_id(1)

    def norm(x):
        ms = jnp.mean(x * x, axis=-1, keepdims=True)
        return x * lax.rsqrt(ms + RMS_EPS) * g_ref[...]

    prev = norm(hp_ref[0])
    xn_ref[0:FFN_HALO, :] = jnp.where(i > 0, prev, jnp.zeros_like(prev)).astype(xn_ref.dtype)
    x = h_ref[0]
    xn_ref[FFN_HALO:FFN_HALO + FFN_TILE, :] = norm(x).astype(xn_ref.dtype)
    xn = xn_ref[...]

    def conv(c0):
        hcol = _dot(xn, wu_ref[:, c0:c0 + FFN_CHUNK])
        out = jnp.broadcast_to(cb_ref[:, c0:c0 + FFN_CHUNK], (FFN_TILE, FFN_CHUNK))
        for j in range(FFN_CONV_WIDTH):
            back = FFN_CONV_WIDTH - 1 - j
            shifted = hcol if back == 0 else pltpu.roll(hcol, back, 0)
            out = out + cw_ref[j:j + 1, c0:c0 + FFN_CHUNK] * shifted[FFN_HALO:FFN_HALO + FFN_TILE]
        return out

    for c0 in range(0, FFN_HIDDEN, FFN_CHUNK):
        gate = conv(c0)
        up = conv(FFN_HIDDEN + c0)
        act_ref[:, c0:c0 + FFN_CHUNK] = (gate * _sigmoid(gate) * up).astype(act_ref.dtype)
    y = x + _dot(act_ref[...], wd_ref[...])
    if final_norm:
        y = y * lax.rsqrt(jnp.mean(y * y, axis=-1, keepdims=True) + RMS_EPS) * fg_ref[...]
    out_ref[0] = y


def _ffn(h, g, wu, cw, cb, wd, final_g=None):
    B, Tp, D = h.shape
    final_norm = final_g is not None
    row = pl.BlockSpec((1, FFN_TILE, D), lambda b, i: (b, i, 0))
    halo_blocks = FFN_TILE // FFN_HALO
    halo = pl.BlockSpec((1, FFN_HALO, D),
                        lambda b, i: (b, jnp.maximum(i * halo_blocks - 1, 0), 0))
    const = lambda shape: pl.BlockSpec(shape, lambda b, i: (0,) * len(shape),
                                       pipeline_mode=pl.Buffered(1))
    operands = [h, h, g, wu, cw, cb, wd] + ([final_g] if final_norm else [])
    return pl.pallas_call(
        functools.partial(_ffn_kernel, final_norm=final_norm),
        grid=(B, Tp // FFN_TILE),
        in_specs=[row, halo, const((1, D)), const((D, 2 * FFN_HIDDEN)),
                  const((FFN_CONV_WIDTH, 2 * FFN_HIDDEN)), const((1, 2 * FFN_HIDDEN)),
                  const((FFN_HIDDEN, D))] + ([const((1, D))] if final_norm else []),
        out_specs=row,
        out_shape=jax.ShapeDtypeStruct((B, Tp, D), jnp.float32),
        scratch_shapes=[pltpu.VMEM((FFN_HALO + FFN_TILE, D), MXU_DTYPE),
                        pltpu.VMEM((FFN_TILE, FFN_HIDDEN), MXU_DTYPE)],
        compiler_params=pltpu.CompilerParams(
            dimension_semantics=("arbitrary", "arbitrary"), vmem_limit_bytes=VMEM_LIMIT),
        name="ffn",
    )(*operands)


def _rope_tables(Tp):
    half = ROT_DIM // 2
    pos = jnp.arange(Tp, dtype=jnp.float32)
    inv_freq = jnp.power(jnp.float32(ROPE_THETA),
                         -jnp.arange(0, ROT_DIM, 2, dtype=jnp.float32) / ROT_DIM)
    ang = pos[:, None] * inv_freq[None, :]
    cos, sin = jnp.cos(ang), jnp.sin(ang)
    zeros = lambda n: jnp.zeros((Tp, n), jnp.float32)
    cos_h = jnp.concatenate([cos, cos, jnp.ones((Tp, HEAD_DIM - ROT_DIM), jnp.float32)], axis=1)
    sina_h = jnp.concatenate([-sin, zeros(HEAD_DIM - half)], axis=1)
    sinb_h = jnp.concatenate([zeros(half), sin, zeros(HEAD_DIM - ROT_DIM)], axis=1)
    two = lambda t: jnp.concatenate([t, t], axis=1)
    return two(cos_h), two(sina_h), two(sinb_h)


def _pack_w_in(w):
    sizes = (N_HEADS * HEAD_DIM, N_KV_HEADS * HEAD_DIM, N_KV_HEADS * HEAD_DIM,
             IDX_HEADS * IDX_DIM, IDX_DIM, IDX_HEADS, 2 * D_MODEL, D_MODEL, D_MODEL)
    offs = [0]
    for s in sizes:
        offs.append(offs[-1] + s)
    wq, wk, wv, wqi, wki, wwi, wglu, wga, wgc = (w[:, offs[n]:offs[n + 1]] for n in range(9))

    def dup_heads(m):
        n = m.shape[1] // HEAD_DIM
        m = m.reshape(m.shape[0], n, 1, HEAD_DIM)
        return jnp.broadcast_to(m, (m.shape[0], n, 2, HEAD_DIM)).reshape(m.shape[0], n * LANES)

    pad_lanes = lambda m: jnp.pad(m, ((0, 0), (0, LANES - m.shape[1])))
    q_scale = (HEAD_DIM ** -0.5) * math.log2(math.e)
    cols = [wq * q_scale, wqi * (IDX_DIM ** -0.5), dup_heads(wk), pad_lanes(wki), wv,
            pad_lanes(wwi), wglu, wga, wgc]
    return jnp.concatenate(cols, axis=1).astype(MXU_DTYPE)


def kernel(x, meta_tokens, attn_norm_g, w_in, w_attn_out, conv_dw_w, conv_dw_b, conv_ln_g,
           conv_ln_b, w_conv_out, w_o, ffn_norm_g, w_up, ffn_dw_w, ffn_dw_b, w_down,
           final_norm_g):
    B, S, D = x.shape
    T = N_META + S
    Tp = -(-T // ROW_TILE) * ROW_TILE
    bf = MXU_DTYPE
    meta = jnp.broadcast_to(meta_tokens[None].astype(x.dtype), (B, N_META, D))
    h = jnp.concatenate([meta, x, jnp.zeros((B, Tp - T, D), x.dtype)], axis=1)
    cos, sina, sinb = _rope_tables(Tp)
    tri = (lax.broadcasted_iota(jnp.int32, (K_TILE, K_TILE), 1)
           < lax.broadcasted_iota(jnp.int32, (K_TILE, K_TILE), 0)).astype(bf)
    row = lambda v: v.reshape(1, -1)
    depth = w_in.shape[0]
    assert depth >= 1
    for l in range(depth):
        qt, qit, k, ki, vt, wit, u, ga, gc, nb = _proj(h, row(attn_norm_g[l]), _pack_w_in(w_in[l]),
                                                       cos, sina, sinb)
        o = _attn(qt, qit, wit, k, ki, vt, tri, nb)
        h = _mix(o, u, ga, gc, h, w_attn_out[l].astype(bf), w_conv_out[l].astype(bf),
                 w_o[l].astype(bf), conv_dw_w[l], row(conv_dw_b[l]), row(conv_ln_g[l]),
                 row(conv_ln_b[l]))
        h = _ffn(h, row(ffn_norm_g[l]), w_up[l].astype(bf), ffn_dw_w[l], row(ffn_dw_b[l]),
                 w_down[l].astype(bf), final_g=row(final_norm_g) if l == depth - 1 else None)
    return h[:, N_META:T]
```
